```python
import jax
import jax.numpy as jnp
from jax import lax
import numpy as np

D_MODEL = 2048
BATCH = 2
SEQ = 4096
DEPTH = 1

ATT_HEADS = 8
ATT_KV_HEADS = 2
HEAD_DIM = 128
IDX_HEADS = 8
IDX_DIM = 64
TOPK_MAX = 256
Q_BLOCK = 128
ROPE_THETA = 10000.0
ATT_SCALE = HEAD_DIM ** -0.5
IDX_SCALE = (IDX_HEADS * IDX_DIM) ** -0.5
HG_HEADS = 8
HG_KEY_DIM = 128
HG_VAL_DIM = 128
HG_CHUNK = 64
N_GROUPS = 4
EXPERTS_PER_GROUP = 8
N_EXPERTS = N_GROUPS * EXPERTS_PER_GROUP
EXPERT_FF = D_MODEL // 4
EPS = 1e-6

IN_SPLITS = (ATT_HEADS * HEAD_DIM, ATT_KV_HEADS * HEAD_DIM, ATT_KV_HEADS * HEAD_DIM,
             IDX_HEADS * IDX_DIM, IDX_DIM, IDX_HEADS,
             HG_HEADS * HG_KEY_DIM, HG_HEADS * HG_KEY_DIM, HG_HEADS * HG_VAL_DIM, HG_HEADS * HG_VAL_DIM,
             D_MODEL, D_MODEL)
IN_WIDTH = sum(IN_SPLITS)

kernel_name = 'hybrid_dsa_hgrn2_hmoe_block'


def rmsnorm(x, g):
    xf = x.astype(jnp.float32)
    y = xf * lax.rsqrt(jnp.mean(xf * xf, axis=-1, keepdims=True) + EPS)
    return (y * g.astype(jnp.float32)).astype(x.dtype)


def ada_norm(x, g, shift, scale):
    return rmsnorm(x, g) * (1 + scale[:, None]) + shift[:, None]


def split_cols(z):
    parts = []
    off = 0
    for w in IN_SPLITS:
        parts.append(z[..., off:off + w])
        off += w
    return parts


def rope_tables(positions, dim):
    inv = jnp.power(ROPE_THETA, -jnp.arange(0, dim, 2, dtype=jnp.float32) / dim)
    ang = positions.astype(jnp.float32)[..., None] * inv
    return jnp.cos(ang), jnp.sin(ang)


def apply_rope(x, cos, sin):
    xf = x.astype(jnp.float32)
    half = x.shape[-1] // 2
    x1, x2 = xf[..., :half], xf[..., half:]
    c, s = cos[:, :, None], sin[:, :, None]
    return jnp.concatenate([x1 * c - x2 * s, x2 * c + x1 * s], axis=-1).astype(x.dtype)


def dsa_attention(q, k, v, q_idx, k_idx, w_idx):
    b, s = q.shape[0], q.shape[1]
    n_blk = s // Q_BLOCK
    topk = min(TOPK_MAX, s // 4)
    k_idx32 = k_idx.astype(jnp.float32)
    key_pos = jnp.arange(s)

    def to_blocks(a):
        return jnp.moveaxis(a.reshape((b, n_blk, Q_BLOCK) + a.shape[2:]), 1, 0)

    def block(args):
        qb, qib, wib, start = args
        t = start + jnp.arange(Q_BLOCK)
        causal = key_pos[None, :] <= t[:, None]
        dots = jnp.einsum('bqhd,bsd->bqhs', qib.astype(jnp.float32), k_idx32)
        score = jnp.einsum('bqh,bqhs->bqs', wib.astype(jnp.float32) * IDX_SCALE, jax.nn.relu(dots))
        score = jnp.where(causal[None], score, -jnp.inf)
        _, sel = lax.top_k(score, topk)
        valid = sel <= t[None, :, None]
        k_sel = jax.vmap(lambda kb, ib: kb[ib])(k, sel)
        v_sel = jax.vmap(lambda vb, ib: vb[ib])(v, sel)
        logits = jnp.einsum('bqhgd,bqkhd->bhgqk', qb.astype(jnp.float32), k_sel.astype(jnp.float32)) * ATT_SCALE
        logits = jnp.where(valid[:, None, None], logits, -jnp.inf)
        p = jax.nn.softmax(logits, axis=-1)
        return jnp.einsum('bhgqk,bqkhd->bqhgd', p.astype(v.dtype), v_sel)

    starts = jnp.arange(n_blk) * Q_BLOCK
    out = lax.map(block, (to_blocks(q), to_blocks(q_idx), to_blocks(w_idx), starts))
    return jnp.moveaxis(out, 0, 1).reshape(b, s, -1)


def hgrn2_recurrence(q, f_logit, i, lb):
    b, s, h, dk = q.shape
    dv = i.shape[-1]
    n_c = s // HG_CHUNK
    f32 = jnp.float32
    lb = lb.astype(f32)
    q = jax.nn.silu(q.astype(f32))
    f = lb + (1.0 - lb) * jax.nn.sigmoid(f_logit.astype(f32))
    log_f = jnp.log(f)
    k = 1.0 - f

    def chunks(a):
        return a.reshape((b, n_c, HG_CHUNK) + a.shape[2:])

    q, log_f, k, v = chunks(q), chunks(log_f), chunks(k), chunks(i.astype(f32))
    cum = jnp.cumsum(log_f, axis=2)
    cum_last = cum[:, :, -1]
    q_dec = q * jnp.exp(cum)
    k_dec = k * jnp.exp(-cum)
    att = jnp.einsum('bnthk,bnshk->bnhts', q_dec, k_dec)
    tril = jnp.tril(jnp.ones((HG_CHUNK, HG_CHUNK), dtype=bool))
    att = jnp.where(tril, att, 0.0)
    intra = jnp.einsum('bnhts,bnshv->bnthv', att, v)
    k_to_end = k * jnp.exp(cum_last[:, :, None] - cum)
    upd = jnp.einsum('bnshk,bnshv->bnhkv', k_to_end, v)

    def step(state, inp):
        decay, u = inp
        return decay[..., None] * state + u, state

    init = jnp.zeros((b, h, dk, dv), f32)
    _, s_prev = lax.scan(step, init, (jnp.moveaxis(jnp.exp(cum_last), 1, 0), jnp.moveaxis(upd, 1, 0)))
    s_prev = jnp.moveaxis(s_prev, 0, 1)
    inter = jnp.einsum('bnthk,bnhkv->bnthv', q_dec, s_prev)
    return (intra + inter).reshape(b, s, h, dv)


def hier_moe(h, w_rg, b_rg, w_re, b_re, w_gate, w_up, w_down):
    b, s, _ = h.shape
    g_logit = (h @ w_rg).astype(jnp.float32) + b_rg.astype(jnp.float32)
    p_grp, grp = lax.top_k(jax.nn.softmax(g_logit, axis=-1), 1)
    e_logit = ((h @ w_re).astype(jnp.float32) + b_re.astype(jnp.float32)).reshape(b, s, N_GROUPS, EXPERTS_PER_GROUP)
    e_in_grp = jnp.take_along_axis(e_logit, grp[..., None], axis=2)[:, :, 0]
    p_e, e_idx = lax.top_k(jax.nn.softmax(e_in_grp, axis=-1), 2)
    p_e = p_e / jnp.sum(p_e, axis=-1, keepdims=True)
    w_tok = p_grp * p_e
    comb = jnp.sum(jax.nn.one_hot(e_idx, EXPERTS_PER_GROUP, dtype=jnp.float32) * w_tok[..., None], axis=2)
    y = jnp.zeros_like(h)
    for gi in range(N_GROUPS):
        cg = jnp.where(grp == gi, comb, 0.0).astype(h.dtype)
        lo, hi = gi * EXPERTS_PER_GROUP, (gi + 1) * EXPERTS_PER_GROUP
        a = jnp.einsum('bsd,edf->bsef', h, w_gate[lo:hi])
        u = jnp.einsum('bsd,edf->bsef', h, w_up[lo:hi])
        y = y + jnp.einsum('bsef,efd->bsd', jax.nn.silu(a) * u * cg[..., None], w_down[lo:hi])
    return y


def setup_inputs(seed: int = 0) -> dict:
    key = jax.random.key(seed)
    ks = jax.random.split(key, 24)
    f32 = jnp.float32

    def nrm(k, shape, scale):
        return jax.random.normal(k, shape, f32) * scale

    d = D_MODEL
    hgw = HG_HEADS * HG_KEY_DIM
    return {
        'x': nrm(ks[0], (BATCH, SEQ, d), 1.0),
        'c': nrm(ks[1], (BATCH, d), 1.0),
        'positions': (jnp.arange(SEQ, dtype=jnp.int32)[None, :]
                      + jax.random.randint(ks[2], (BATCH, 1), 0, 1024, dtype=jnp.int32)),
        'w_ada': nrm(ks[3], (DEPTH, d, 6 * d), 0.5 * d ** -0.5),
        'b_ada': nrm(ks[4], (DEPTH, 6 * d), 0.02),
        'g_norm1': 1.0 + nrm(ks[5], (DEPTH, d), 0.02),
        'w_in': nrm(ks[6], (DEPTH, d, IN_WIDTH), d ** -0.5),
        'g_head': 1.0 + nrm(ks[7], (DEPTH, HG_HEADS, HG_VAL_DIM), 0.02),
        'hg_lower_bounds': 1.0 + nrm(ks[8], (DEPTH + 1, hgw), 0.1),
        'w_attn_up': nrm(ks[9], (DEPTH, ATT_HEADS * HEAD_DIM, d), (ATT_HEADS * HEAD_DIM) ** -0.5),
        'w_hgrn_up': nrm(ks[10], (DEPTH, HG_HEADS * HG_VAL_DIM, d), (HG_HEADS * HG_VAL_DIM) ** -0.5),
        'w_out': nrm(ks[11], (DEPTH, d, d), d ** -0.5),
        'g_norm2': 1.0 + nrm(ks[12], (DEPTH, d), 0.02),
        'w_router_group': nrm(ks[13], (DEPTH, d, N_GROUPS), d ** -0.5),
        'b_router_group': nrm(ks[14], (DEPTH, N_GROUPS), 0.01),
        'w_router_expert': nrm(ks[15], (DEPTH, d, N_EXPERTS), d ** -0.5),
        'b_router_expert': nrm(ks[16], (DEPTH, N_EXPERTS), 0.01),
        'w_exp_gate': nrm(ks[17], (DEPTH, N_EXPERTS, d, EXPERT_FF), d ** -0.5),
        'w_exp_up': nrm(ks[18], (DEPTH, N_EXPERTS, d, EXPERT_FF), d ** -0.5),
        'w_exp_down': nrm(ks[19], (DEPTH, N_EXPERTS, EXPERT_FF, d), EXPERT_FF ** -0.5),
        'g_final': 1.0 + nrm(ks[20], (d,), 0.02),
    }


def reference(x, c, positions, w_ada, b_ada, g_norm1, w_in, g_head, hg_lower_bounds, w_attn_up, w_hgrn_up,
              w_out, g_norm2, w_router_group, b_router_group, w_router_expert, b_router_expert,
              w_exp_gate, w_exp_up, w_exp_down, g_final):
    b, s, _ = x.shape
    cos_a, sin_a = rope_tables(positions, HEAD_DIM)
    cos_i, sin_i = rope_tables(positions, IDX_DIM)
    lb_all = jnp.cumsum(jax.nn.softmax(hg_lower_bounds.astype(jnp.float32), axis=0), axis=0)
    c_act = jax.nn.silu(c)
    for l in range(DEPTH):
        mod = c_act @ w_ada[l] + b_ada[l]
        sh1, sc1, gt1, sh2, sc2, gt2 = jnp.split(mod, 6, axis=-1)
        h = ada_norm(x, g_norm1[l], sh1, sc1)
        aq, ak, av, iq, ik, iw, hq, hf, hi, hog, ga, gh = split_cols(h @ w_in[l])
        aq = apply_rope(aq.reshape(b, s, ATT_HEADS, HEAD_DIM), cos_a, sin_a)
        aq = aq.reshape(b, s, ATT_KV_HEADS, ATT_HEADS // ATT_KV_HEADS, HEAD_DIM)
        ak = apply_rope(ak.reshape(b, s, ATT_KV_HEADS, HEAD_DIM), cos_a, sin_a)
        av = av.reshape(b, s, ATT_KV_HEADS, HEAD_DIM)
        iq = apply_rope(iq.reshape(b, s, IDX_HEADS, IDX_DIM), cos_i, sin_i)
        ik = apply_rope(ik[:, :, None], cos_i, sin_i)[:, :, 0]
        y_att = dsa_attention(aq, ak, av, iq, ik, iw)
        o_h = hgrn2_recurrence(hq.reshape(b, s, HG_HEADS, HG_KEY_DIM),
                               hf.reshape(b, s, HG_HEADS, HG_KEY_DIM),
                               hi.reshape(b, s, HG_HEADS, HG_VAL_DIM),
                               lb_all[l].reshape(HG_HEADS, HG_KEY_DIM))
        y_hg = rmsnorm(o_h, g_head[l]).astype(x.dtype).reshape(b, s, -1) * jax.nn.silu(hog)
        merged = (jax.nn.sigmoid(ga) * (y_att @ w_attn_up[l])
                  + jax.nn.sigmoid(gh) * (y_hg @ w_hgrn_up[l]))
        x = x + gt1[:, None] * (merged @ w_out[l])
        h2 = ada_norm(x, g_norm2[l], sh2, sc2)
        x = x + gt2[:, None] * hier_moe(h2, w_router_group[l], b_router_group[l], w_router_expert[l],
                                        b_router_expert[l], w_exp_gate[l], w_exp_up[l], w_exp_down[l])
    return rmsnorm(x, g_final)
```

```python
import functools
import math

import jax
import jax.numpy as jnp
from jax import lax
from jax.experimental import pallas as pl
from jax.experimental.pallas import tpu as pltpu

F32 = jnp.float32
BF16 = jnp.bfloat16
I32 = jnp.int32

ATT_HEADS = 8
ATT_KV_HEADS = 2
ATT_GROUP = ATT_HEADS // ATT_KV_HEADS
HEAD_DIM = 128
IDX_HEADS = 8
IDX_DIM = 64
TOPK_MAX = 256
Q_BLOCK = 128
ROPE_THETA = 10000.0
ATT_SCALE = HEAD_DIM ** -0.5
IDX_SCALE = (IDX_HEADS * IDX_DIM) ** -0.5
HG_HEADS = 8
HG_DIM = 128
HG_CHUNK = 64
N_GROUPS = 4
EXPERTS_PER_GROUP = 8
N_EXPERTS = N_GROUPS * EXPERTS_PER_GROUP
EPS = 1e-6

LANES = 128
INT_MIN = -(2 ** 31)
NEG_INF_KEY = INT_MIN + 0x7FFFFF
MIB = 1024 * 1024

AQ_OFF = 0
AK_OFF = ATT_HEADS * HEAD_DIM
AV_OFF = AK_OFF + ATT_KV_HEADS * HEAD_DIM
IQ_OFF = AV_OFF + ATT_KV_HEADS * HEAD_DIM
IK_OFF = IQ_OFF + IDX_HEADS * IDX_DIM
ATT_COLS = IK_OFF + IDX_DIM + IDX_HEADS
ATT_WIDTH = IK_OFF + LANES


def _sigmoid(x):
    return 1.0 / (1.0 + jnp.exp(-x))


def _silu(x):
    return x * _sigmoid(x)


def _params(sem, vmem_mib):
    return pltpu.CompilerParams(dimension_semantics=sem, vmem_limit_bytes=vmem_mib * MIB)


def _mod_kernel(c_ref, w_ref, b_ref, o_ref):
    c = c_ref[...]
    ca = _silu(c).astype(BF16)
    o_ref[...] = jnp.dot(ca, w_ref[...].astype(BF16), preferred_element_type=F32) + b_ref[...]


def _modulation(c, w, b):
    bsz, d = c.shape
    n = w.shape[1]
    rows = 8
    cp = jnp.zeros((rows, d), F32).at[:bsz].set(c)
    tn = 1024
    out = pl.pallas_call(
        _mod_kernel,
        grid=(n // tn,),
        in_specs=[pl.BlockSpec((rows, d), lambda j: (0, 0)),
                  pl.BlockSpec((d, tn), lambda j: (0, j)),
                  pl.BlockSpec((1, tn), lambda j: (0, j))],
        out_specs=pl.BlockSpec((rows, tn), lambda j: (0, j)),
        out_shape=jax.ShapeDtypeStruct((rows, n), F32),
        compiler_params=_params(("arbitrary",), 40),
        name="modulation",
    )(cp, w, b.reshape(1, n))
    return out[:bsz].reshape(bsz, 6, d)


def _ada_norm(x, g, shift, scale):
    ms = jnp.mean(x * x, axis=-1, keepdims=True)
    y = x * lax.rsqrt(ms + EPS) * g
    return y * (1.0 + scale) + shift


def _norm_matmul_kernel(x_ref, mod_ref, g_ref, w_ref, o_ref, h_ref):
    @pl.when(pl.program_id(2) == 0)
    def _():
        h = _ada_norm(x_ref[...], g_ref[...], mod_ref[0:1, :], mod_ref[1:2, :])
        h_ref[...] = h.astype(BF16)

    o_ref[...] = jnp.dot(h_ref[...], w_ref[...], preferred_element_type=F32)


def _norm_matmul(x, mod, g, w, tn):
    bsz, s, d = x.shape
    n = w.shape[1]
    tm = min(512, s)
    return pl.pallas_call(
        _norm_matmul_kernel,
        grid=(bsz, s // tm, n // tn),
        in_specs=[pl.BlockSpec((None, tm, d), lambda b, i, j: (b, i, 0)),
                  pl.BlockSpec((None, 6, d), lambda b, i, j: (b, 0, 0)),
                  pl.BlockSpec((1, d), lambda b, i, j: (0, 0)),
                  pl.BlockSpec((d, tn), lambda b, i, j: (0, j))],
        out_specs=pl.BlockSpec((None, tm, tn), lambda b, i, j: (b, i, j)),
        out_shape=jax.ShapeDtypeStruct((bsz, s, n), F32),
        scratch_shapes=[pltpu.VMEM((tm, d), BF16)],
        compiler_params=_params(("arbitrary", "arbitrary", "arbitrary"), 48),
        name="norm_matmul",
    )(x, mod, g.reshape(1, d), w)


def _sortable(score):
    bits = pltpu.bitcast(score + 0.0, I32)
    return jnp.where(bits < 0, bits ^ 0x7FFFFFFF, bits)


def _attn_kernel(pos_ref, z_ref, o_ref, k_s, vt_s, ik_s, key_s, bias_s, lg_s, acc_s, thr_s, *, topk):
    i = pl.program_id(1)
    n_chunks = i + 1
    qb = Q_BLOCK
    r0 = pl.multiple_of(i * qb, qb)

    lane = lax.broadcasted_iota(I32, (1, LANES), 1)
    pos = pos_ref[...].astype(F32)
    inv_a = jnp.exp((lane % (HEAD_DIM // 2)).astype(F32) * (-2.0 * math.log(ROPE_THETA) / HEAD_DIM))
    ang_a = pos * inv_a
    cos_a = jnp.cos(ang_a)
    sin_a = jnp.where(lane < HEAD_DIM // 2, -1.0, 1.0) * jnp.sin(ang_a)
    inv_i = jnp.exp((lane % (IDX_DIM // 2)).astype(F32) * (-2.0 * math.log(ROPE_THETA) / IDX_DIM))
    ang_i = pos * inv_i
    low_i = (lane % IDX_DIM) < IDX_DIM // 2
    cos_i = jnp.cos(ang_i)
    sin_i = jnp.where(low_i, -1.0, 1.0) * jnp.sin(ang_i)

    def rope_a(x):
        return x * cos_a + pltpu.roll(x, HEAD_DIM // 2, 1) * sin_a

    def rope_i(x):
        partner = jnp.where(low_i, pltpu.roll(x, LANES - IDX_DIM // 2, 1), pltpu.roll(x, IDX_DIM // 2, 1))
        return x * cos_i + partner * sin_i

    for h in range(ATT_KV_HEADS):
        kh = rope_a(z_ref[:, AK_OFF + h * HEAD_DIM:AK_OFF + (h + 1) * HEAD_DIM])
        k_s[pl.ds(r0, qb), h * HEAD_DIM:(h + 1) * HEAD_DIM] = kh.astype(BF16)
    v = z_ref[:, AV_OFF:AV_OFF + ATT_KV_HEADS * HEAD_DIM]
    vt_s[i] = v.T.astype(BF16)
    tile = z_ref[:, IK_OFF:IK_OFF + LANES]
    ikr = rope_i(tile)
    ik2 = jnp.where(lane < IDX_DIM, ikr, pltpu.roll(ikr, IDX_DIM, 1))
    ik_s[pl.ds(r0, qb), :] = ik2.astype(BF16)
    w_t = tile.T[IDX_DIM:IDX_DIM + IDX_HEADS, :] * IDX_SCALE

    q_rows = []
    for t in range(IDX_HEADS // 2):
        iqr = rope_i(z_ref[:, IQ_OFF + t * LANES:IQ_OFF + (t + 1) * LANES])
        q_rows.append(jnp.where(lane < IDX_DIM, iqr, 0.0).astype(BF16))
        q_rows.append(jnp.where(lane < IDX_DIM, 0.0, iqr).astype(BF16))
    iq_all = jnp.concatenate(q_rows, axis=0)

    key_iota = lax.broadcasted_iota(I32, (qb, LANES), 0)
    causal = key_iota <= lane

    def score_body(j, carry):
        kc = ik_s[pl.ds(pl.multiple_of(j * qb, qb), qb), :]
        dots = lax.dot_general(kc, iq_all, (((1,), (1,)), ((), ())), preferred_element_type=F32)
        sc = jnp.zeros((qb, LANES), F32)
        for h in range(IDX_HEADS):
            sc = sc + jnp.maximum(dots[:, h * qb:(h + 1) * qb], 0.0) * w_t[h:h + 1, :]
        sc = jnp.where(jnp.logical_or(j < i, causal), sc, -jnp.inf)
        key_s[pl.ds(pl.multiple_of(j * qb, qb), qb), :] = _sortable(sc)
        return carry

    lax.fori_loop(0, n_chunks, score_body, 0)

    thr_s[...] = jnp.full((1, LANES), NEG_INF_KEY + 1, I32)
    select = i >= topk // qb

    @pl.when(select)
    def _():
        def bit_body(b, t_u):
            cand = t_u | jnp.left_shift(jnp.int32(1), 31 - b)
            cand_s = cand ^ INT_MIN

            def cnt_body(j, acc):
                kk = key_s[pl.ds(pl.multiple_of(j * qb, qb), qb), :]
                return acc + jnp.where(kk >= cand_s, 1, 0)

            acc = lax.fori_loop(0, n_chunks, cnt_body, jnp.zeros((qb, LANES), I32))
            cnt = jnp.sum(acc, axis=0, keepdims=True)
            return jnp.where(cnt >= topk, cand, t_u)

        t_u = lax.fori_loop(0, 32, bit_body, jnp.zeros((1, LANES), I32))
        thr_s[...] = t_u ^ INT_MIN

    thr = thr_s[...]

    def bias_body(j, cnt):
        rows = pl.ds(pl.multiple_of(j * qb, qb), qb)
        sel = key_s[rows, :] >= thr
        bias_s[rows, :] = jnp.where(sel, 0.0, -jnp.inf)
        return cnt + jnp.where(sel, 1, 0)

    cnt_ge = jnp.sum(lax.fori_loop(0, n_chunks, bias_body, jnp.zeros((qb, LANES), I32)), axis=0, keepdims=True)
    has_ties = jnp.logical_and(select, jnp.max(cnt_ge) > topk)

    @pl.when(has_ties)
    def _():
        n_keep = (topk - (cnt_ge - _count_eq(key_s, thr, n_chunks, qb))).astype(F32)
        tril = (lax.broadcasted_iota(I32, (qb, qb), 1) <= lax.broadcasted_iota(I32, (qb, qb), 0)).astype(BF16)

        def tie_body(j, run):
            rows = pl.ds(pl.multiple_of(j * qb, qb), qb)
            kk = key_s[rows, :]
            eq = kk == thr
            pref = jnp.dot(tril, jnp.where(eq, 1.0, 0.0).astype(BF16), preferred_element_type=F32) + run
            sel = jnp.logical_or(kk > thr, jnp.logical_and(eq, pref <= n_keep))
            bias_s[rows, :] = jnp.where(sel, 0.0, -jnp.inf)
            return pref[qb - 1:qb, :]

        lax.fori_loop(0, n_chunks, tie_body, jnp.zeros((1, LANES), F32))

    gw = ATT_GROUP * qb
    for h in range(ATT_KV_HEADS):
        qs = jnp.concatenate(
            [(rope_a(z_ref[:, AQ_OFF + (h * ATT_GROUP + g) * HEAD_DIM:AQ_OFF + (h * ATT_GROUP + g + 1) * HEAD_DIM])
              * ATT_SCALE).astype(BF16) for g in range(ATT_GROUP)], axis=0)

        def logit_body(j, m8):
            rows = pl.ds(pl.multiple_of(j * qb, qb), qb)
            kc = k_s[rows, h * HEAD_DIM:(h + 1) * HEAD_DIM]
            lt = lax.dot_general(kc, qs, (((1,), (1,)), ((), ())), preferred_element_type=F32)
            bias = bias_s[rows, :]
            lm = lt + jnp.concatenate([bias] * ATT_GROUP, axis=1)
            lg_s[rows, :] = lm
            return jnp.maximum(m8, jnp.max(lm.reshape(qb // 8, 8, gw), axis=0))

        m8 = lax.fori_loop(0, n_chunks, logit_body, jnp.full((8, gw), -jnp.inf, F32))
        m = jnp.max(m8, axis=0, keepdims=True)
        acc_s[...] = jnp.zeros_like(acc_s)

        def pv_body(j, l8):
            rows = pl.ds(pl.multiple_of(j * qb, qb), qb)
            p = jnp.exp(lg_s[rows, :] - m)
            vt = vt_s[j][h * HEAD_DIM:(h + 1) * HEAD_DIM, :]
            acc_s[...] += jnp.dot(vt, p.astype(BF16), preferred_element_type=F32)
            return l8 + jnp.sum(p.reshape(qb // 8, 8, gw), axis=0)

        l8 = lax.fori_loop(0, n_chunks, pv_body, jnp.zeros((8, gw), F32))
        o_t = acc_s[...] / jnp.sum(l8, axis=0, keepdims=True)
        for g in range(ATT_GROUP):
            c0 = (h * ATT_GROUP + g) * HEAD_DIM
            o_ref[:, c0:c0 + HEAD_DIM] = o_t[:, g * qb:(g + 1) * qb].T.astype(o_ref.dtype)


def _count_eq(key_s, thr, n_chunks, qb):
    def body(j, acc):
        kk = key_s[pl.ds(pl.multiple_of(j * qb, qb), qb), :]
        return acc + jnp.where(kk == thr, 1, 0)

    return jnp.sum(lax.fori_loop(0, n_chunks, body, jnp.zeros((qb, LANES), I32)), axis=0, keepdims=True)


def _attention(z_att, positions):
    bsz, s, _ = z_att.shape
    nblk = s // Q_BLOCK
    topk = min(TOPK_MAX, s // 4)
    assert s % Q_BLOCK == 0 and topk % Q_BLOCK == 0
    return pl.pallas_call(
        functools.partial(_attn_kernel, topk=topk),
        grid=(bsz, nblk),
        in_specs=[pl.BlockSpec((None, Q_BLOCK, 1), lambda b, i: (b, i, 0)),
                  pl.BlockSpec((None, Q_BLOCK, ATT_WIDTH), lambda b, i: (b, i, 0))],
        out_specs=pl.BlockSpec((None, Q_BLOCK, ATT_HEADS * HEAD_DIM), lambda b, i: (b, i, 0)),
        out_shape=jax.ShapeDtypeStruct((bsz, s, ATT_HEADS * HEAD_DIM), BF16),
        scratch_shapes=[pltpu.VMEM((s, ATT_KV_HEADS * HEAD_DIM), BF16),
                        pltpu.VMEM((nblk, ATT_KV_HEADS * HEAD_DIM, Q_BLOCK), BF16),
                        pltpu.VMEM((s, LANES), BF16),
                        pltpu.VMEM((s, LANES), I32),
                        pltpu.VMEM((s, LANES), F32),
                        pltpu.VMEM((s, ATT_GROUP * Q_BLOCK), F32),
                        pltpu.VMEM((HEAD_DIM, ATT_GROUP * Q_BLOCK), F32),
                        pltpu.VMEM((1, LANES), I32)],
        compiler_params=_params(("arbitrary", "arbitrary"), 48),
        name="dsa_attention",
    )(positions.reshape(bsz, s, 1), z_att)


def _hgrn_kernel(lbp_ref, gh_ref, q_ref, f_ref, i_ref, og_ref, o_ref, *, layer, n_chunks):
    h = pl.program_id(1)
    c = HG_CHUNK
    lbp = lbp_ref[:, pl.ds(h, 1), :]
    e = jnp.exp(lbp - jnp.max(lbp, axis=0, keepdims=True))
    sm = e / jnp.sum(e, axis=0, keepdims=True)
    lb = sm[0]
    for r in range(1, layer + 1):
        lb = lb + sm[r]
    g_head = gh_ref[pl.ds(h, 1), :]

    t_io = lax.broadcasted_iota(I32, (c, c), 0)
    s_io = lax.broadcasted_iota(I32, (c, c), 1)
    tril = s_io <= t_io
    tril_f = tril.astype(F32)

    def chunk_body(n, st):
        rows = pl.ds(pl.multiple_of(n * c, c), c)
        q = _silu(q_ref[rows, :])
        f = lb + (1.0 - lb) * _sigmoid(f_ref[rows, :])
        log_f = jnp.log(f)
        k = 1.0 - f
        v = i_ref[rows, :]
        cum = jnp.dot(tril_f, log_f, preferred_element_type=F32, precision=lax.Precision.HIGHEST)
        cum_last = cum[c - 1:c, :]
        q_dec = (q * jnp.exp(cum)).astype(BF16)
        k_dec = (k * jnp.exp(-cum)).astype(BF16)
        att = lax.dot_general(q_dec, k_dec, (((1,), (1,)), ((), ())), preferred_element_type=F32)
        att = jnp.where(tril, att, 0.0)
        v16 = v.astype(BF16)
        intra = jnp.dot(att.astype(BF16), v16, preferred_element_type=F32)
        inter = lax.dot_general(q_dec, st.astype(BF16), (((1,), (1,)), ((), ())), preferred_element_type=F32)
        o = intra + inter
        y = o * lax.rsqrt(jnp.mean(o * o, axis=-1, keepdims=True) + EPS) * g_head
        o_ref[rows, :] = (y * _silu(og_ref[rows, :])).astype(o_ref.dtype)
        k_end = (k * jnp.exp(cum_last - cum)).astype(BF16)
        upd_t = jnp.dot(v.T.astype(BF16), k_end, preferred_element_type=F32)
        return jnp.exp(cum_last) * st + upd_t

    lax.fori_loop(0, n_chunks, chunk_body, jnp.zeros((HG_DIM, HG_DIM), F32))


def _hgrn(z_hg, lower_bounds, g_head, layer):
    bsz, s, _ = z_hg.shape
    depth1 = lower_bounds.shape[0]
    col = lambda sec: (lambda b, h: (b, 0, sec * HG_HEADS + h))
    specs = [pl.BlockSpec((None, s, HG_DIM), col(sec)) for sec in range(4)]
    return pl.pallas_call(
        functools.partial(_hgrn_kernel, layer=layer, n_chunks=s // HG_CHUNK),
        grid=(bsz, HG_HEADS),
        in_specs=[pl.BlockSpec((depth1, HG_HEADS, HG_DIM), lambda b, h: (0, 0, 0)),
                  pl.BlockSpec((HG_HEADS, HG_DIM), lambda b, h: (0, 0))] + specs,
        out_specs=pl.BlockSpec((None, s, HG_DIM), lambda b, h: (b, 0, h)),
        out_shape=jax.ShapeDtypeStruct((bsz, s, HG_HEADS * HG_DIM), BF16),
        compiler_params=_params(("arbitrary", "arbitrary"), 48),
        name="hgrn2",
    )(lower_bounds.reshape(depth1, HG_HEADS, HG_DIM), g_head, z_hg, z_hg, z_hg, z_hg)


def _merge_kernel(ya_ref, yh_ref, ga_ref, gh_ref, x_ref, mod_ref, g2_ref, wa_ref, wh_ref, wo_ref, wr_ref, br_ref,
                  x1_ref, h2_ref, cw_ref):
    a = jnp.dot(ya_ref[...], wa_ref[...], preferred_element_type=F32)
    hg = jnp.dot(yh_ref[...], wh_ref[...], preferred_element_type=F32)
    merged = _sigmoid(ga_ref[...]) * a + _sigmoid(gh_ref[...]) * hg
    x1 = x_ref[...] + mod_ref[2:3, :] * jnp.dot(merged.astype(BF16), wo_ref[...], preferred_element_type=F32)
    x1_ref[...] = x1
    h2 = _ada_norm(x1, g2_ref[...], mod_ref[3:4, :], mod_ref[4:5, :])
    h2_ref[...] = h2.astype(BF16)

    lg = jnp.dot(h2, wr_ref[...], preferred_element_type=F32, precision=lax.Precision.HIGHEST) + br_ref[...]
    lane = lax.broadcasted_iota(I32, lg.shape, 1)
    neg = -jnp.inf
    is_grp = jnp.logical_and(lane >= N_EXPERTS, lane < N_EXPERTS + N_GROUPS)
    g_lg = jnp.where(is_grp, lg, neg)
    g_max = jnp.max(g_lg, axis=-1, keepdims=True)
    p_grp = 1.0 / jnp.sum(jnp.exp(g_lg - g_max), axis=-1, keepdims=True)
    grp = jnp.min(jnp.where(g_lg == g_max, lane, 2 * LANES), axis=-1, keepdims=True) - N_EXPERTS
    lo = grp * EXPERTS_PER_GROUP
    in_grp = jnp.logical_and(lane >= lo, lane < lo + EXPERTS_PER_GROUP)
    e_lg = jnp.where(in_grp, lg, neg)
    e_max = jnp.max(e_lg, axis=-1, keepdims=True)
    idx1 = jnp.min(jnp.where(e_lg == e_max, lane, 2 * LANES), axis=-1, keepdims=True)
    e_lg2 = jnp.where(lane == idx1, neg, e_lg)
    e_max2 = jnp.max(e_lg2, axis=-1, keepdims=True)
    idx2 = jnp.min(jnp.where(e_lg2 == e_max2, lane, 2 * LANES), axis=-1, keepdims=True)
    e2 = jnp.exp(e_max2 - e_max)
    p1 = p_grp / (1.0 + e2)
    p2 = p_grp * e2 / (1.0 + e2)
    cw_ref[...] = jnp.where(lane == idx1, p1, jnp.where(lane == idx2, p2, 0.0))


def _merge(y_att, y_hg, z_gate, x, mod, g2, wa, wh, wo, wr, br):
    bsz, s, d = x.shape
    tm = min(256, s)
    dh = y_att.shape[-1]
    const = lambda shape: pl.BlockSpec(shape, lambda b, i: (0,) * len(shape))
    row = lambda width, j=0: pl.BlockSpec((None, tm, width), lambda b, i: (b, i, j))
    return pl.pallas_call(
        _merge_kernel,
        grid=(bsz, s // tm),
        in_specs=[row(dh), row(dh), row(d, 0), row(d, 1), row(d),
                  pl.BlockSpec((None, 6, d), lambda b, i: (b, 0, 0)), const((1, d)),
                  const((dh, d)), const((dh, d)), const((d, d)), const((d, LANES)), const((1, LANES))],
        out_specs=[row(d), row(d), row(LANES)],
        out_shape=[jax.ShapeDtypeStruct((bsz, s, d), F32),
                   jax.ShapeDtypeStruct((bsz, s, d), BF16),
                   jax.ShapeDtypeStruct((bsz, s, LANES), F32)],
        compiler_params=_params(("arbitrary", "arbitrary"), 56),
        name="merge_router",
    )(y_att, y_hg, z_gate, z_gate, x, mod, g2.reshape(1, d), wa, wh, wo, wr, br)


def _moe_kernel(h_ref, cw_ref, x1_ref, mod_ref, gf_ref, wg_ref, wu_ref, wd_ref, o_ref, acc_ref, *, gate_row):
    e = pl.program_id(2)

    @pl.when(e == 0)
    def _():
        acc_ref[...] = jnp.zeros_like(acc_ref)

    h = h_ref[...]
    a = jnp.dot(h, wg_ref[...], preferred_element_type=F32)
    u = jnp.dot(h, wu_ref[...], preferred_element_type=F32)
    cw = cw_ref[...]
    lane = lax.broadcasted_iota(I32, cw.shape, 1)
    cwe = jnp.sum(jnp.where(lane == e, cw, 0.0), axis=-1, keepdims=True)
    hm = (_silu(a) * u * cwe).astype(BF16)
    acc_ref[...] += jnp.dot(hm, wd_ref[...], preferred_element_type=F32)

    @pl.when(e == pl.num_programs(2) - 1)
    def _():
        x2 = x1_ref[...] + mod_ref[gate_row:gate_row + 1, :] * acc_ref[...]
        ms = jnp.mean(x2 * x2, axis=-1, keepdims=True)
        o_ref[...] = x2 * lax.rsqrt(ms + EPS) * gf_ref[...]


def _moe(h2, cw, x1, mod, g_out, wg, wu, wd, final_norm):
    bsz, s, d = x1.shape
    ne, _, ff = wg.shape
    tm = min(512, s)
    row = lambda width: pl.BlockSpec((None, tm, width), lambda b, i, e: (b, i, 0))
    return pl.pallas_call(
        functools.partial(_moe_kernel, gate_row=5),
        grid=(bsz, s // tm, ne),
        in_specs=[row(d), row(LANES), row(d),
                  pl.BlockSpec((None, 6, d), lambda b, i, e: (b, 0, 0)),
                  pl.BlockSpec((1, d), lambda b, i, e: (0, 0)),
                  pl.BlockSpec((None, d, ff), lambda b, i, e: (e, 0, 0)),
                  pl.BlockSpec((None, d, ff), lambda b, i, e: (e, 0, 0)),
                  pl.BlockSpec((None, ff, d), lambda b, i, e: (e, 0, 0))],
        out_specs=row(d),
        out_shape=jax.ShapeDtypeStruct((bsz, s, d), F32),
        scratch_shapes=[pltpu.VMEM((tm, d), F32)],
        compiler_params=_params(("arbitrary", "arbitrary", "arbitrary"), 56),
        name="moe_experts",
    )(h2, cw, x1, mod, g_out.reshape(1, d), wg, wu, wd)


def kernel(x, c, positions, w_ada, b_ada, g_norm1, w_in, g_head, hg_lower_bounds, w_attn_up, w_hgrn_up, w_out,
           g_norm2, w_router_group, b_router_group, w_router_expert, b_router_expert, w_exp_gate, w_exp_up,
           w_exp_down, g_final):
    depth = w_ada.shape[0]
    d = x.shape[-1]
    assert depth == 1
    hg_w = HG_HEADS * HG_DIM
    for l in range(depth):
        mod = _modulation(c, w_ada[l], b_ada[l])
        w_l = w_in[l]
        w_att = jnp.pad(w_l[:, :ATT_COLS], ((0, 0), (0, ATT_WIDTH - ATT_COLS))).astype(BF16)
        w_hg = w_l[:, ATT_COLS:ATT_COLS + 4 * hg_w].astype(BF16)
        w_gate = w_l[:, ATT_COLS + 4 * hg_w:].astype(BF16)
        z_att = _norm_matmul(x, mod, g_norm1[l], w_att, ATT_WIDTH)
        z_hg = _norm_matmul(x, mod, g_norm1[l], w_hg, 1024)
        z_gate = _norm_matmul(x, mod, g_norm1[l], w_gate, 1024)
        y_att = _attention(z_att, positions)
        y_hg = _hgrn(z_hg, hg_lower_bounds, g_head[l], l)
        w_r = jnp.pad(jnp.concatenate([w_router_expert[l], w_router_group[l]], axis=1),
                      ((0, 0), (0, LANES - N_EXPERTS - N_GROUPS)))
        b_r = jnp.pad(jnp.concatenate([b_router_expert[l], b_router_group[l]]),
                      (0, LANES - N_EXPERTS - N_GROUPS)).reshape(1, LANES)
        x1, h2, cw = _merge(y_att, y_hg, z_gate, x, mod, g_norm2[l],
                            w_attn_up[l].astype(BF16), w_hgrn_up[l].astype(BF16), w_out[l].astype(BF16), w_r, b_r)
        x = _moe(h2, cw, x1, mod, g_final, w_exp_gate[l].astype(BF16), w_exp_up[l].astype(BF16),
                 w_exp_down[l].astype(BF16), True)
    return x
```

```python
import functools
import math

import jax
import jax.numpy as jnp
from jax import lax
from jax.experimental import pallas as pl
from jax.experimental.pallas import tpu as pltpu

F32 = jnp.float32
BF16 = jnp.bfloat16
I32 = jnp.int32

ATT_HEADS = 8
ATT_KV_HEADS = 2
ATT_GROUP = ATT_HEADS // ATT_KV_HEADS
HEAD_DIM = 128
IDX_HEADS = 8
IDX_DIM = 64
TOPK_MAX = 256
Q_BLOCK = 128
KEY_CHUNK = 256
ROPE_THETA = 10000.0
ATT_SCALE = HEAD_DIM ** -0.5
IDX_SCALE = (IDX_HEADS * IDX_DIM) ** -0.5
HG_HEADS = 8
HG_DIM = 128
HG_CHUNK = 64
N_GROUPS = 4
EXPERTS_PER_GROUP = 8
N_EXPERTS = N_GROUPS * EXPERTS_PER_GROUP
EPS = 1e-6

LANES = 128
INT_MIN = -(2 ** 31)
NEG_INF_KEY = INT_MIN + 0x7FFFFF
MIB = 1024 * 1024

AQ_OFF = 0
AK_OFF = ATT_HEADS * HEAD_DIM
AV_OFF = AK_OFF + ATT_KV_HEADS * HEAD_DIM
IQ_OFF = AV_OFF + ATT_KV_HEADS * HEAD_DIM
IK_OFF = IQ_OFF + IDX_HEADS * IDX_DIM
ATT_COLS = IK_OFF + IDX_DIM + IDX_HEADS
ATT_WIDTH = IK_OFF + LANES


def _sigmoid(x):
    return 1.0 / (1.0 + jnp.exp(-x))


def _silu(x):
    return x * _sigmoid(x)


def _params(sem, vmem_mib):
    return pltpu.CompilerParams(dimension_semantics=sem, vmem_limit_bytes=vmem_mib * MIB)


def _mod_kernel(c_ref, w_ref, b_ref, o_ref):
    c = c_ref[...]
    ca = _silu(c).astype(BF16)
    o_ref[...] = jnp.dot(ca, w_ref[...].astype(BF16), preferred_element_type=F32) + b_ref[...]


def _modulation(c, w, b):
    bsz, d = c.shape
    n = w.shape[1]
    rows = 8
    cp = jnp.zeros((rows, d), F32).at[:bsz].set(c)
    tn = 1024
    out = pl.pallas_call(
        _mod_kernel,
        grid=(n // tn,),
        in_specs=[pl.BlockSpec((rows, d), lambda j: (0, 0)),
                  pl.BlockSpec((d, tn), lambda j: (0, j)),
                  pl.BlockSpec((1, tn), lambda j: (0, j))],
        out_specs=pl.BlockSpec((rows, tn), lambda j: (0, j)),
        out_shape=jax.ShapeDtypeStruct((rows, n), F32),
        compiler_params=_params(("arbitrary",), 40),
        name="modulation",
    )(cp, w, b.reshape(1, n))
    return out[:bsz].reshape(bsz, 6, d)


def _ada_norm(x, g, shift, scale):
    ms = jnp.mean(x * x, axis=-1, keepdims=True)
    y = x * lax.rsqrt(ms + EPS) * g
    return y * (1.0 + scale) + shift


def _norm_matmul_kernel(x_ref, mod_ref, g_ref, w_ref, o_ref, h_ref):
    @pl.when(pl.program_id(2) == 0)
    def _():
        h = _ada_norm(x_ref[...], g_ref[...], mod_ref[0:1, :], mod_ref[1:2, :])
        h_ref[...] = h.astype(BF16)

    o_ref[...] = jnp.dot(h_ref[...], w_ref[...], preferred_element_type=F32)


def _norm_matmul(x, mod, g, w, tn):
    bsz, s, d = x.shape
    n = w.shape[1]
    tm = min(512, s)
    return pl.pallas_call(
        _norm_matmul_kernel,
        grid=(bsz, s // tm, n // tn),
        in_specs=[pl.BlockSpec((None, tm, d), lambda b, i, j: (b, i, 0)),
                  pl.BlockSpec((None, 6, d), lambda b, i, j: (b, 0, 0)),
                  pl.BlockSpec((1, d), lambda b, i, j: (0, 0)),
                  pl.BlockSpec((d, tn), lambda b, i, j: (0, j))],
        out_specs=pl.BlockSpec((None, tm, tn), lambda b, i, j: (b, i, j)),
        out_shape=jax.ShapeDtypeStruct((bsz, s, n), F32),
        scratch_shapes=[pltpu.VMEM((tm, d), BF16)],
        compiler_params=_params(("arbitrary", "arbitrary", "arbitrary"), 48),
        name="norm_matmul",
    )(x, mod, g.reshape(1, d), w)


def _sortable(score):
    bits = pltpu.bitcast(score + 0.0, I32)
    return jnp.where(bits < 0, bits ^ 0x7FFFFFFF, bits)


def _attn_kernel(pos_ref, z_ref, o_ref, k_s, vt_s, ik_s, key_s, bias_s, lg_s, acc_s, thr_s, *, topk):
    i = pl.program_id(1)
    qb = Q_BLOCK
    kc = KEY_CHUNK
    n_chunks = (i * qb + kc) // kc
    r0 = pl.multiple_of(i * qb, qb)

    @pl.when(jnp.logical_and(pl.program_id(0) == 0, i == 0))
    def _():
        k_s[...] = jnp.zeros_like(k_s)
        vt_s[...] = jnp.zeros_like(vt_s)
        ik_s[...] = jnp.zeros_like(ik_s)

    lane = lax.broadcasted_iota(I32, (1, LANES), 1)
    pos = pos_ref[...].astype(F32)
    inv_a = jnp.exp((lane % (HEAD_DIM // 2)).astype(F32) * (-2.0 * math.log(ROPE_THETA) / HEAD_DIM))
    ang_a = pos * inv_a
    cos_a = jnp.cos(ang_a)
    sin_a = jnp.where(lane < HEAD_DIM // 2, -1.0, 1.0) * jnp.sin(ang_a)
    inv_i = jnp.exp((lane % (IDX_DIM // 2)).astype(F32) * (-2.0 * math.log(ROPE_THETA) / IDX_DIM))
    ang_i = pos * inv_i
    low_i = (lane % IDX_DIM) < IDX_DIM // 2
    cos_i = jnp.cos(ang_i)
    sin_i = jnp.where(low_i, -1.0, 1.0) * jnp.sin(ang_i)

    def rope_a(x):
        return x * cos_a + pltpu.roll(x, HEAD_DIM // 2, 1) * sin_a

    def rope_i(x):
        partner = jnp.where(low_i, pltpu.roll(x, LANES - IDX_DIM // 2, 1), pltpu.roll(x, IDX_DIM // 2, 1))
        return x * cos_i + partner * sin_i

    for h in range(ATT_KV_HEADS):
        kh = rope_a(z_ref[:, AK_OFF + h * HEAD_DIM:AK_OFF + (h + 1) * HEAD_DIM])
        k_s[pl.ds(r0, qb), h * HEAD_DIM:(h + 1) * HEAD_DIM] = kh.astype(BF16)
    v = z_ref[:, AV_OFF:AV_OFF + ATT_KV_HEADS * HEAD_DIM]
    vt_s[i] = v.T.astype(BF16)
    tile = z_ref[:, IK_OFF:IK_OFF + LANES]
    ikr = rope_i(tile)
    ik2 = jnp.where(lane < IDX_DIM, ikr, pltpu.roll(ikr, IDX_DIM, 1))
    ik_s[pl.ds(r0, qb), :] = ik2.astype(BF16)
    w_t = tile.T[IDX_DIM:IDX_DIM + IDX_HEADS, :] * IDX_SCALE

    q_rows = []
    for t in range(IDX_HEADS // 2):
        iqr = rope_i(z_ref[:, IQ_OFF + t * LANES:IQ_OFF + (t + 1) * LANES])
        q_rows.append(jnp.where(lane < IDX_DIM, iqr, 0.0).astype(BF16))
        q_rows.append(jnp.where(lane < IDX_DIM, 0.0, iqr).astype(BF16))
    iq_all = jnp.concatenate(q_rows, axis=0)

    key_iota = lax.broadcasted_iota(I32, (kc, LANES), 0)
    q_idx = i * qb + lane

    def chunk_rows(j):
        return pl.ds(pl.multiple_of(j * kc, kc), kc)

    def score_body(j, carry):
        dots = lax.dot_general(ik_s[chunk_rows(j), :], iq_all, (((1,), (1,)), ((), ())),
                               preferred_element_type=F32)
        sc = jnp.zeros((kc, LANES), F32)
        for h in range(IDX_HEADS):
            sc = sc + jnp.maximum(dots[:, h * qb:(h + 1) * qb], 0.0) * w_t[h:h + 1, :]
        sc = jnp.where(key_iota + j * kc <= q_idx, sc, -jnp.inf)
        key_s[chunk_rows(j), :] = _sortable(sc)
        return carry

    lax.fori_loop(0, n_chunks, score_body, 0)

    thr_s[...] = jnp.full((1, LANES), NEG_INF_KEY + 1, I32)
    select = i >= topk // qb

    @pl.when(select)
    def _():
        def bit_body(b, t_u):
            cand = t_u | jnp.left_shift(jnp.int32(1), 31 - b)
            cand_s = cand ^ INT_MIN

            def cnt_body(j, acc):
                return acc + jnp.where(key_s[chunk_rows(j), :] >= cand_s, 1, 0)

            acc = lax.fori_loop(0, n_chunks, cnt_body, jnp.zeros((kc, LANES), I32))
            cnt = jnp.sum(acc, axis=0, keepdims=True)
            return jnp.where(cnt >= topk, cand, t_u)

        t_u = lax.fori_loop(0, 32, bit_body, jnp.zeros((1, LANES), I32))
        thr_s[...] = t_u ^ INT_MIN

    thr = thr_s[...]

    def bias_body(j, cnt):
        sel = key_s[chunk_rows(j), :] >= thr
        bias_s[chunk_rows(j), :] = jnp.where(sel, 0.0, -jnp.inf)
        return cnt + jnp.where(sel, 1, 0)

    cnt_ge = jnp.sum(lax.fori_loop(0, n_chunks, bias_body, jnp.zeros((kc, LANES), I32)), axis=0, keepdims=True)
    has_ties = jnp.logical_and(select, jnp.max(cnt_ge) > topk)

    @pl.when(has_ties)
    def _():
        def eq_body(j, acc):
            return acc + jnp.where(key_s[chunk_rows(j), :] == thr, 1, 0)

        cnt_eq = jnp.sum(lax.fori_loop(0, n_chunks, eq_body, jnp.zeros((kc, LANES), I32)), axis=0, keepdims=True)
        n_keep = (topk - (cnt_ge - cnt_eq)).astype(F32)
        tril = (lax.broadcasted_iota(I32, (kc, kc), 1) <= lax.broadcasted_iota(I32, (kc, kc), 0)).astype(BF16)

        def tie_body(j, run):
            kk = key_s[chunk_rows(j), :]
            eq = kk == thr
            pref = jnp.dot(tril, jnp.where(eq, 1.0, 0.0).astype(BF16), preferred_element_type=F32) + run
            sel = jnp.logical_or(kk > thr, jnp.logical_and(eq, pref <= n_keep))
            bias_s[chunk_rows(j), :] = jnp.where(sel, 0.0, -jnp.inf)
            return pref[kc - 1:kc, :]

        lax.fori_loop(0, n_chunks, tie_body, jnp.zeros((1, LANES), F32))

    gw = ATT_GROUP * qb
    for h in range(ATT_KV_HEADS):
        qs = jnp.concatenate(
            [(rope_a(z_ref[:, AQ_OFF + (h * ATT_GROUP + g) * HEAD_DIM:AQ_OFF + (h * ATT_GROUP + g + 1) * HEAD_DIM])
              * ATT_SCALE).astype(BF16) for g in range(ATT_GROUP)], axis=0)

        def logit_body(j, m8):
            kch = k_s[chunk_rows(j), h * HEAD_DIM:(h + 1) * HEAD_DIM]
            lt = lax.dot_general(kch, qs, (((1,), (1,)), ((), ())), preferred_element_type=F32)
            bias = bias_s[chunk_rows(j), :]
            lm = lt + jnp.concatenate([bias] * ATT_GROUP, axis=1)
            lg_s[chunk_rows(j), :] = lm
            return jnp.maximum(m8, jnp.max(lm.reshape(kc // 8, 8, gw), axis=0))

        m8 = lax.fori_loop(0, n_chunks, logit_body, jnp.full((8, gw), -jnp.inf, F32))
        m = jnp.max(m8, axis=0, keepdims=True)
        acc_s[...] = jnp.zeros_like(acc_s)

        def pv_body(j, l8):
            p = jnp.exp(lg_s[chunk_rows(j), :] - m)
            vt = jnp.concatenate([vt_s[(kc // qb) * j + t][h * HEAD_DIM:(h + 1) * HEAD_DIM, :]
                                  for t in range(kc // qb)], axis=1)
            acc_s[...] += jnp.dot(vt, p.astype(BF16), preferred_element_type=F32)
            return l8 + jnp.sum(p.reshape(kc // 8, 8, gw), axis=0)

        l8 = lax.fori_loop(0, n_chunks, pv_body, jnp.zeros((8, gw), F32))
        o_t = acc_s[...] / jnp.sum(l8, axis=0, keepdims=True)
        for g in range(ATT_GROUP):
            c0 = (h * ATT_GROUP + g) * HEAD_DIM
            o_ref[:, c0:c0 + HEAD_DIM] = o_t[:, g * qb:(g + 1) * qb].T.astype(o_ref.dtype)


def _attention(z_att, positions):
    bsz, s, _ = z_att.shape
    nblk = s // Q_BLOCK
    topk = min(TOPK_MAX, s // 4)
    assert s % KEY_CHUNK == 0 and KEY_CHUNK % Q_BLOCK == 0 and topk % Q_BLOCK == 0
    return pl.pallas_call(
        functools.partial(_attn_kernel, topk=topk),
        grid=(bsz, nblk),
        in_specs=[pl.BlockSpec((None, Q_BLOCK, 1), lambda b, i: (b, i, 0)),
                  pl.BlockSpec((None, Q_BLOCK, ATT_WIDTH), lambda b, i: (b, i, 0))],
        out_specs=pl.BlockSpec((None, Q_BLOCK, ATT_HEADS * HEAD_DIM), lambda b, i: (b, i, 0)),
        out_shape=jax.ShapeDtypeStruct((bsz, s, ATT_HEADS * HEAD_DIM), BF16),
        scratch_shapes=[pltpu.VMEM((s, ATT_KV_HEADS * HEAD_DIM), BF16),
                        pltpu.VMEM((nblk, ATT_KV_HEADS * HEAD_DIM, Q_BLOCK), BF16),
                        pltpu.VMEM((s, LANES), BF16),
                        pltpu.VMEM((s, LANES), I32),
                        pltpu.VMEM((s, LANES), F32),
                        pltpu.VMEM((s, ATT_GROUP * Q_BLOCK), F32),
                        pltpu.VMEM((HEAD_DIM, ATT_GROUP * Q_BLOCK), F32),
                        pltpu.VMEM((1, LANES), I32)],
        compiler_params=_params(("arbitrary", "arbitrary"), 48),
        name="dsa_attention",
    )(positions.reshape(bsz, s, 1), z_att)


def _hgrn_kernel(lbp_ref, gh_ref, q_ref, f_ref, i_ref, og_ref, o_ref, *, layer, n_chunks):
    h = pl.program_id(1)
    c = HG_CHUNK
    lbp = lbp_ref[:, pl.ds(h, 1), :]
    e = jnp.exp(lbp - jnp.max(lbp, axis=0, keepdims=True))
    sm = e / jnp.sum(e, axis=0, keepdims=True)
    lb = sm[0]
    for r in range(1, layer + 1):
        lb = lb + sm[r]
    g_head = gh_ref[pl.ds(h, 1), :]

    t_io = lax.broadcasted_iota(I32, (c, c), 0)
    s_io = lax.broadcasted_iota(I32, (c, c), 1)
    tril = s_io <= t_io
    tril_f = tril.astype(F32)

    def chunk_body(n, st):
        rows = pl.ds(pl.multiple_of(n * c, c), c)
        q = _silu(q_ref[rows, :])
        f = lb + (1.0 - lb) * _sigmoid(f_ref[rows, :])
        log_f = jnp.log(f)
        k = 1.0 - f
        v = i_ref[rows, :]
        cum = jnp.dot(tril_f, log_f, preferred_element_type=F32, precision=lax.Precision.HIGHEST)
        cum_last = cum[c - 1:c, :]
        q_dec = (q * jnp.exp(cum)).astype(BF16)
        k_dec = (k * jnp.exp(-cum)).astype(BF16)
        att = lax.dot_general(q_dec, k_dec, (((1,), (1,)), ((), ())), preferred_element_type=F32)
        att = jnp.where(tril, att, 0.0)
        v16 = v.astype(BF16)
        intra = jnp.dot(att.astype(BF16), v16, preferred_element_type=F32)
        inter = lax.dot_general(q_dec, st.astype(BF16), (((1,), (1,)), ((), ())), preferred_element_type=F32)
        o = intra + inter
        y = o * lax.rsqrt(jnp.mean(o * o, axis=-1, keepdims=True) + EPS) * g_head
        o_ref[rows, :] = (y * _silu(og_ref[rows, :])).astype(o_ref.dtype)
        k_end = (k * jnp.exp(cum_last - cum)).astype(BF16)
        upd_t = jnp.dot(v.T.astype(BF16), k_end, preferred_element_type=F32)
        return jnp.exp(cum_last) * st + upd_t

    lax.fori_loop(0, n_chunks, chunk_body, jnp.zeros((HG_DIM, HG_DIM), F32), unroll=2)


def _hgrn(z_hg, lower_bounds, g_head, layer):
    bsz, s, _ = z_hg.shape
    depth1 = lower_bounds.shape[0]
    col = lambda sec: (lambda b, h: (b, 0, sec * HG_HEADS + h))
    specs = [pl.BlockSpec((None, s, HG_DIM), col(sec)) for sec in range(4)]
    return pl.pallas_call(
        functools.partial(_hgrn_kernel, layer=layer, n_chunks=s // HG_CHUNK),
        grid=(bsz, HG_HEADS),
        in_specs=[pl.BlockSpec((depth1, HG_HEADS, HG_DIM), lambda b, h: (0, 0, 0)),
                  pl.BlockSpec((HG_HEADS, HG_DIM), lambda b, h: (0, 0))] + specs,
        out_specs=pl.BlockSpec((None, s, HG_DIM), lambda b, h: (b, 0, h)),
        out_shape=jax.ShapeDtypeStruct((bsz, s, HG_HEADS * HG_DIM), BF16),
        compiler_params=_params(("arbitrary", "arbitrary"), 48),
        name="hgrn2",
    )(lower_bounds.reshape(depth1, HG_HEADS, HG_DIM), g_head, z_hg, z_hg, z_hg, z_hg)


def _merge_kernel(ya_ref, yh_ref, ga_ref, gh_ref, x_ref, mod_ref, g2_ref, wa_ref, wh_ref, wo_ref, wr_ref, br_ref,
                  x1_ref, h2_ref, cw_ref):
    a = jnp.dot(ya_ref[...], wa_ref[...], preferred_element_type=F32)
    hg = jnp.dot(yh_ref[...], wh_ref[...], preferred_element_type=F32)
    merged = _sigmoid(ga_ref[...]) * a + _sigmoid(gh_ref[...]) * hg
    x1 = x_ref[...] + mod_ref[2:3, :] * jnp.dot(merged.astype(BF16), wo_ref[...], preferred_element_type=F32)
    x1_ref[...] = x1
    h2 = _ada_norm(x1, g2_ref[...], mod_ref[3:4, :], mod_ref[4:5, :])
    h2_ref[...] = h2.astype(BF16)

    lg = jnp.dot(h2, wr_ref[...], preferred_element_type=F32, precision=lax.Precision.HIGHEST) + br_ref[...]
    lane = lax.broadcasted_iota(I32, lg.shape, 1)
    neg = -jnp.inf
    is_grp = jnp.logical_and(lane >= N_EXPERTS, lane < N_EXPERTS + N_GROUPS)
    g_lg = jnp.where(is_grp, lg, neg)
    g_max = jnp.max(g_lg, axis=-1, keepdims=True)
    p_grp = 1.0 / jnp.sum(jnp.exp(g_lg - g_max), axis=-1, keepdims=True)
    grp = jnp.min(jnp.where(g_lg == g_max, lane, 2 * LANES), axis=-1, keepdims=True) - N_EXPERTS
    lo = grp * EXPERTS_PER_GROUP
    in_grp = jnp.logical_and(lane >= lo, lane < lo + EXPERTS_PER_GROUP)
    e_lg = jnp.where(in_grp, lg, neg)
    e_max = jnp.max(e_lg, axis=-1, keepdims=True)
    idx1 = jnp.min(jnp.where(e_lg == e_max, lane, 2 * LANES), axis=-1, keepdims=True)
    e_lg2 = jnp.where(lane == idx1, neg, e_lg)
    e_max2 = jnp.max(e_lg2, axis=-1, keepdims=True)
    idx2 = jnp.min(jnp.where(e_lg2 == e_max2, lane, 2 * LANES), axis=-1, keepdims=True)
    e2 = jnp.exp(e_max2 - e_max)
    p1 = p_grp / (1.0 + e2)
    p2 = p_grp * e2 / (1.0 + e2)
    cw_ref[...] = jnp.where(lane == idx1, p1, jnp.where(lane == idx2, p2, 0.0))


def _merge(y_att, y_hg, z_gate, x, mod, g2, wa, wh, wo, wr, br):
    bsz, s, d = x.shape
    tm = min(256, s)
    dh = y_att.shape[-1]
    const = lambda shape: pl.BlockSpec(shape, lambda b, i: (0,) * len(shape))
    row = lambda width, j=0: pl.BlockSpec((None, tm, width), lambda b, i: (b, i, j))
    return pl.pallas_call(
        _merge_kernel,
        grid=(bsz, s // tm),
        in_specs=[row(dh), row(dh), row(d, 0), row(d, 1), row(d),
                  pl.BlockSpec((None, 6, d), lambda b, i: (b, 0, 0)), const((1, d)),
                  const((dh, d)), const((dh, d)), const((d, d)), const((d, LANES)), const((1, LANES))],
        out_specs=[row(d), row(d), row(LANES)],
        out_shape=[jax.ShapeDtypeStruct((bsz, s, d), F32),
                   jax.ShapeDtypeStruct((bsz, s, d), BF16),
                   jax.ShapeDtypeStruct((bsz, s, LANES), F32)],
        compiler_params=_params(("arbitrary", "arbitrary"), 56),
        name="merge_router",
    )(y_att, y_hg, z_gate, z_gate, x, mod, g2.reshape(1, d), wa, wh, wo, wr, br)


def _moe_kernel(h_ref, cw_ref, x1_ref, mod_ref, gf_ref, wg_ref, wu_ref, wd_ref, o_ref, acc_ref, *, gate_row):
    e = pl.program_id(2)

    @pl.when(e == 0)
    def _():
        acc_ref[...] = jnp.zeros_like(acc_ref)

    h = h_ref[...]
    a = jnp.dot(h, wg_ref[...], preferred_element_type=F32)
    u = jnp.dot(h, wu_ref[...], preferred_element_type=F32)
    cw = cw_ref[...]
    lane = lax.broadcasted_iota(I32, cw.shape, 1)
    cwe = jnp.sum(jnp.where(lane == e, cw, 0.0), axis=-1, keepdims=True)
    hm = (_silu(a) * u * cwe).astype(BF16)
    acc_ref[...] += jnp.dot(hm, wd_ref[...], preferred_element_type=F32)

    @pl.when(e == pl.num_programs(2) - 1)
    def _():
        x2 = x1_ref[...] + mod_ref[gate_row:gate_row + 1, :] * acc_ref[...]
        ms = jnp.mean(x2 * x2, axis=-1, keepdims=True)
        o_ref[...] = x2 * lax.rsqrt(ms + EPS) * gf_ref[...]


def _moe(h2, cw, x1, mod, g_out, wg, wu, wd, final_norm):
    bsz, s, d = x1.shape
    ne, _, ff = wg.shape
    tm = min(512, s)
    row = lambda width: pl.BlockSpec((None, tm, width), lambda b, i, e: (b, i, 0))
    return pl.pallas_call(
        functools.partial(_moe_kernel, gate_row=5),
        grid=(bsz, s // tm, ne),
        in_specs=[row(d), row(LANES), row(d),
                  pl.BlockSpec((None, 6, d), lambda b, i, e: (b, 0, 0)),
                  pl.BlockSpec((1, d), lambda b, i, e: (0, 0)),
                  pl.BlockSpec((None, d, ff), lambda b, i, e: (e, 0, 0)),
                  pl.BlockSpec((None, d, ff), lambda b, i, e: (e, 0, 0)),
                  pl.BlockSpec((None, ff, d), lambda b, i, e: (e, 0, 0))],
        out_specs=row(d),
        out_shape=jax.ShapeDtypeStruct((bsz, s, d), F32),
        scratch_shapes=[pltpu.VMEM((tm, d), F32)],
        compiler_params=_params(("arbitrary", "arbitrary", "arbitrary"), 56),
        name="moe_experts",
    )(h2, cw, x1, mod, g_out.reshape(1, d), wg, wu, wd)


def kernel(x, c, positions, w_ada, b_ada, g_norm1, w_in, g_head, hg_lower_bounds, w_attn_up, w_hgrn_up, w_out,
           g_norm2, w_router_group, b_router_group, w_router_expert, b_router_expert, w_exp_gate, w_exp_up,
           w_exp_down, g_final):
    depth = w_ada.shape[0]
    d = x.shape[-1]
    assert depth == 1
    hg_w = HG_HEADS * HG_DIM
    for l in range(depth):
        mod = _modulation(c, w_ada[l], b_ada[l])
        w_l = w_in[l]
        w_att = jnp.pad(w_l[:, :ATT_COLS], ((0, 0), (0, ATT_WIDTH - ATT_COLS))).astype(BF16)
        w_hg = w_l[:, ATT_COLS:ATT_COLS + 4 * hg_w].astype(BF16)
        w_gate = w_l[:, ATT_COLS + 4 * hg_w:].astype(BF16)
        z_att = _norm_matmul(x, mod, g_norm1[l], w_att, ATT_WIDTH)
        z_hg = _norm_matmul(x, mod, g_norm1[l], w_hg, 1024)
        z_gate = _norm_matmul(x, mod, g_norm1[l], w_gate, 1024)
        y_att = _attention(z_att, positions)
        y_hg = _hgrn(z_hg, hg_lower_bounds, g_head[l], l)
        w_r = jnp.pad(jnp.concatenate([w_router_expert[l], w_router_group[l]], axis=1),
                      ((0, 0), (0, LANES - N_EXPERTS - N_GROUPS)))
        b_r = jnp.pad(jnp.concatenate([b_router_expert[l], b_router_group[l]]),
                      (0, LANES - N_EXPERTS - N_GROUPS)).reshape(1, LANES)
        x1, h2, cw = _merge(y_att, y_hg, z_gate, x, mod, g_norm2[l],
                            w_attn_up[l].astype(BF16), w_hgrn_up[l].astype(BF16), w_out[l].astype(BF16), w_r, b_r)
        x = _moe(h2, cw, x1, mod, g_final, w_exp_gate[l].astype(BF16), w_exp_up[l].astype(BF16),
                 w_exp_down[l].astype(BF16), True)
    return x
```

```python
import functools
import math

import jax
import jax.numpy as jnp
from jax import lax
from jax.experimental import pallas as pl
from jax.experimental.pallas import tpu as pltpu

F32 = jnp.float32
BF16 = jnp.bfloat16
I32 = jnp.int32

ATT_HEADS = 8
ATT_KV_HEADS = 2
ATT_GROUP = ATT_HEADS // ATT_KV_HEADS
HEAD_DIM = 128
IDX_HEADS = 8
IDX_DIM = 64
TOPK_MAX = 256
Q_BLOCK = 128
KEY_CHUNK = 256
ROPE_THETA = 10000.0
ATT_SCALE = HEAD_DIM ** -0.5
IDX_SCALE = (IDX_HEADS * IDX_DIM) ** -0.5
HG_HEADS = 8
HG_DIM = 128
HG_CHUNK = 64
N_GROUPS = 4
EXPERTS_PER_GROUP = 8
N_EXPERTS = N_GROUPS * EXPERTS_PER_GROUP
EPS = 1e-6

LANES = 128
SUBLANES = 8
INT_MIN = -(2 ** 31)
NEG_INF_KEY = INT_MIN + 0x7FFFFF
MIB = 1024 * 1024

AQ_OFF = 0
AK_OFF = ATT_HEADS * HEAD_DIM
AV_OFF = AK_OFF + ATT_KV_HEADS * HEAD_DIM
IQ_OFF = AV_OFF + ATT_KV_HEADS * HEAD_DIM
IK_OFF = IQ_OFF + IDX_HEADS * IDX_DIM
ATT_COLS = IK_OFF + IDX_DIM + IDX_HEADS
ATT_WIDTH = IK_OFF + LANES


def _sigmoid(x):
    return 1.0 / (1.0 + jnp.exp(-x))


def _silu(x):
    return x * _sigmoid(x)


def _params(sem, vmem_mib):
    return pltpu.CompilerParams(dimension_semantics=sem, vmem_limit_bytes=vmem_mib * MIB)


def _mod_kernel(c_ref, w_ref, b_ref, o_ref):
    c = c_ref[...]
    ca = _silu(c).astype(BF16)
    o_ref[...] = jnp.dot(ca, w_ref[...].astype(BF16), preferred_element_type=F32) + b_ref[...]


def _modulation(c, w, b):
    bsz, d = c.shape
    n = w.shape[1]
    rows = 8
    cp = jnp.zeros((rows, d), F32).at[:bsz].set(c)
    tn = 1024
    out = pl.pallas_call(
        _mod_kernel,
        grid=(n // tn,),
        in_specs=[pl.BlockSpec((rows, d), lambda j: (0, 0)),
                  pl.BlockSpec((d, tn), lambda j: (0, j)),
                  pl.BlockSpec((1, tn), lambda j: (0, j))],
        out_specs=pl.BlockSpec((rows, tn), lambda j: (0, j)),
        out_shape=jax.ShapeDtypeStruct((rows, n), F32),
        compiler_params=_params(("arbitrary",), 40),
        name="modulation",
    )(cp, w, b.reshape(1, n))
    return out[:bsz].reshape(bsz, 6, d)


def _ada_norm(x, g, shift, scale):
    ms = jnp.mean(x * x, axis=-1, keepdims=True)
    y = x * lax.rsqrt(ms + EPS) * g
    return y * (1.0 + scale) + shift


def _norm_matmul_kernel(x_ref, mod_ref, g_ref, w_ref, o_ref, h_ref):
    @pl.when(pl.program_id(2) == 0)
    def _():
        h = _ada_norm(x_ref[...], g_ref[...], mod_ref[0:1, :], mod_ref[1:2, :])
        h_ref[...] = h.astype(BF16)

    o_ref[...] = jnp.dot(h_ref[...], w_ref[...], preferred_element_type=F32)


def _norm_matmul(x, mod, g, w, tn):
    bsz, s, d = x.shape
    n = w.shape[1]
    tm = min(512, s)
    return pl.pallas_call(
        _norm_matmul_kernel,
        grid=(bsz, s // tm, n // tn),
        in_specs=[pl.BlockSpec((None, tm, d), lambda b, i, j: (b, i, 0)),
                  pl.BlockSpec((None, 6, d), lambda b, i, j: (b, 0, 0)),
                  pl.BlockSpec((1, d), lambda b, i, j: (0, 0)),
                  pl.BlockSpec((d, tn), lambda b, i, j: (0, j))],
        out_specs=pl.BlockSpec((None, tm, tn), lambda b, i, j: (b, i, j)),
        out_shape=jax.ShapeDtypeStruct((bsz, s, n), F32),
        scratch_shapes=[pltpu.VMEM((tm, d), BF16)],
        compiler_params=_params(("arbitrary", "arbitrary", "arbitrary"), 48),
        name="norm_matmul",
    )(x, mod, g.reshape(1, d), w)


def _sortable(score):
    bits = pltpu.bitcast(score + 0.0, I32)
    return jnp.where(bits < 0, bits ^ 0x7FFFFFFF, bits)


def _attn_kernel(pos_ref, z_ref, o_ref, k_s, vt_s, ik_s, key_s, bias_s, lg_s, acc_s, thr_s, *, topk):
    i = pl.program_id(1)
    qb = Q_BLOCK
    kc = KEY_CHUNK
    n_chunks = (i * qb + kc) // kc
    r0 = pl.multiple_of(i * qb, qb)

    @pl.when(jnp.logical_and(pl.program_id(0) == 0, i == 0))
    def _():
        k_s[...] = jnp.zeros_like(k_s)
        vt_s[...] = jnp.zeros_like(vt_s)
        ik_s[...] = jnp.zeros_like(ik_s)

    lane = lax.broadcasted_iota(I32, (1, LANES), 1)
    pos = pos_ref[...].astype(F32)
    inv_a = jnp.exp((lane % (HEAD_DIM // 2)).astype(F32) * (-2.0 * math.log(ROPE_THETA) / HEAD_DIM))
    ang_a = pos * inv_a
    cos_a = jnp.cos(ang_a)
    sin_a = jnp.where(lane < HEAD_DIM // 2, -1.0, 1.0) * jnp.sin(ang_a)
    inv_i = jnp.exp((lane % (IDX_DIM // 2)).astype(F32) * (-2.0 * math.log(ROPE_THETA) / IDX_DIM))
    ang_i = pos * inv_i
    low_i = (lane % IDX_DIM) < IDX_DIM // 2
    cos_i = jnp.cos(ang_i)
    sin_i = jnp.where(low_i, -1.0, 1.0) * jnp.sin(ang_i)

    def rope_a(x):
        return x * cos_a + pltpu.roll(x, HEAD_DIM // 2, 1) * sin_a

    def rope_i(x):
        partner = jnp.where(low_i, pltpu.roll(x, LANES - IDX_DIM // 2, 1), pltpu.roll(x, IDX_DIM // 2, 1))
        return x * cos_i + partner * sin_i

    for h in range(ATT_KV_HEADS):
        kh = rope_a(z_ref[:, AK_OFF + h * HEAD_DIM:AK_OFF + (h + 1) * HEAD_DIM])
        k_s[pl.ds(r0, qb), h * HEAD_DIM:(h + 1) * HEAD_DIM] = kh.astype(BF16)
    v = z_ref[:, AV_OFF:AV_OFF + ATT_KV_HEADS * HEAD_DIM]
    vt_s[i] = v.T.astype(BF16)
    tile = z_ref[:, IK_OFF:IK_OFF + LANES]
    ikr = rope_i(tile)
    ik2 = jnp.where(lane < IDX_DIM, ikr, pltpu.roll(ikr, IDX_DIM, 1))
    ik_s[pl.ds(r0, qb), :] = ik2.astype(BF16)
    w_t = tile.T[IDX_DIM:IDX_DIM + IDX_HEADS, :] * IDX_SCALE

    q_rows = []
    for t in range(IDX_HEADS // 2):
        iqr = rope_i(z_ref[:, IQ_OFF + t * LANES:IQ_OFF + (t + 1) * LANES])
        q_rows.append(jnp.where(lane < IDX_DIM, iqr, 0.0).astype(BF16))
        q_rows.append(jnp.where(lane < IDX_DIM, 0.0, iqr).astype(BF16))
    iq_all = jnp.concatenate(q_rows, axis=0)

    key_iota = lax.broadcasted_iota(I32, (kc, LANES), 0)
    q_idx = i * qb + lane

    def chunk_rows(j):
        return pl.ds(pl.multiple_of(j * kc, kc), kc)

    def score_body(j, carry):
        dots = lax.dot_general(ik_s[chunk_rows(j), :], iq_all, (((1,), (1,)), ((), ())),
                               preferred_element_type=F32)
        sc = jnp.zeros((kc, LANES), F32)
        for h in range(IDX_HEADS):
            sc = sc + jnp.maximum(dots[:, h * qb:(h + 1) * qb], 0.0) * w_t[h:h + 1, :]
        sc = jnp.where(key_iota + j * kc <= q_idx, sc, -jnp.inf)
        key_s[chunk_rows(j), :] = _sortable(sc)
        return carry

    lax.fori_loop(0, n_chunks, score_body, 0)

    thr_s[...] = jnp.full((1, LANES), NEG_INF_KEY + 1, I32)
    select = i >= topk // qb

    @pl.when(select)
    def _():
        def bit_body(b, t_u):
            cand = t_u | jnp.left_shift(jnp.int32(1), 31 - b)
            cand_s = cand ^ INT_MIN

            def cnt_body(j, acc):
                return acc + jnp.where(key_s[chunk_rows(j), :] >= cand_s, 1, 0)

            acc = lax.fori_loop(0, n_chunks, cnt_body, jnp.zeros((kc, LANES), I32))
            cnt = jnp.sum(acc, axis=0, keepdims=True)
            return jnp.where(cnt >= topk, cand, t_u)

        t_u = lax.fori_loop(0, 32, bit_body, jnp.zeros((1, LANES), I32))
        thr_s[...] = t_u ^ INT_MIN

    thr = thr_s[...]

    def bias_body(j, cnt):
        sel = key_s[chunk_rows(j), :] >= thr
        bias_s[chunk_rows(j), :] = jnp.where(sel, 0.0, -jnp.inf)
        return cnt + jnp.where(sel, 1, 0)

    cnt_ge = jnp.sum(lax.fori_loop(0, n_chunks, bias_body, jnp.zeros((kc, LANES), I32)), axis=0, keepdims=True)
    has_ties = jnp.logical_and(select, jnp.max(cnt_ge) > topk)

    @pl.when(has_ties)
    def _():
        def eq_body(j, acc):
            return acc + jnp.where(key_s[chunk_rows(j), :] == thr, 1, 0)

        cnt_eq = jnp.sum(lax.fori_loop(0, n_chunks, eq_body, jnp.zeros((kc, LANES), I32)), axis=0, keepdims=True)
        n_keep = (topk - (cnt_ge - cnt_eq)).astype(F32)
        tril = (lax.broadcasted_iota(I32, (kc, kc), 1) <= lax.broadcasted_iota(I32, (kc, kc), 0)).astype(BF16)

        def tie_body(j, run):
            kk = key_s[chunk_rows(j), :]
            eq = kk == thr
            pref = jnp.dot(tril, jnp.where(eq, 1.0, 0.0).astype(BF16), preferred_element_type=F32) + run
            sel = jnp.logical_or(kk > thr, jnp.logical_and(eq, pref <= n_keep))
            bias_s[chunk_rows(j), :] = jnp.where(sel, 0.0, -jnp.inf)
            return pref[kc - 1:kc, :]

        lax.fori_loop(0, n_chunks, tie_body, jnp.zeros((1, LANES), F32))

    gw = ATT_GROUP * qb
    for h in range(ATT_KV_HEADS):
        qs = jnp.concatenate(
            [(rope_a(z_ref[:, AQ_OFF + (h * ATT_GROUP + g) * HEAD_DIM:AQ_OFF + (h * ATT_GROUP + g + 1) * HEAD_DIM])
              * ATT_SCALE).astype(BF16) for g in range(ATT_GROUP)], axis=0)

        def logit_body(j, m8):
            kch = k_s[chunk_rows(j), h * HEAD_DIM:(h + 1) * HEAD_DIM]
            lt = lax.dot_general(kch, qs, (((1,), (1,)), ((), ())), preferred_element_type=F32)
            bias = bias_s[chunk_rows(j), :]
            lm = lt + jnp.concatenate([bias] * ATT_GROUP, axis=1)
            lg_s[chunk_rows(j), :] = lm
            return jnp.maximum(m8, jnp.max(lm.reshape(kc // 8, 8, gw), axis=0))

        m8 = lax.fori_loop(0, n_chunks, logit_body, jnp.full((8, gw), -jnp.inf, F32))
        m = jnp.max(m8, axis=0, keepdims=True)
        acc_s[...] = jnp.zeros_like(acc_s)

        def pv_body(j, l8):
            p = jnp.exp(lg_s[chunk_rows(j), :] - m)
            vt = jnp.concatenate([vt_s[(kc // qb) * j + t][h * HEAD_DIM:(h + 1) * HEAD_DIM, :]
                                  for t in range(kc // qb)], axis=1)
            acc_s[...] += jnp.dot(vt, p.astype(BF16), preferred_element_type=F32)
            return l8 + jnp.sum(p.reshape(kc // 8, 8, gw), axis=0)

        l8 = lax.fori_loop(0, n_chunks, pv_body, jnp.zeros((8, gw), F32))
        o_t = acc_s[...] / jnp.sum(l8, axis=0, keepdims=True)
        for g in range(ATT_GROUP):
            c0 = (h * ATT_GROUP + g) * HEAD_DIM
            o_ref[:, c0:c0 + HEAD_DIM] = o_t[:, g * qb:(g + 1) * qb].T.astype(o_ref.dtype)


def _attention(z_att, positions):
    bsz, s, _ = z_att.shape
    nblk = s // Q_BLOCK
    topk = min(TOPK_MAX, s // 4)
    assert s % KEY_CHUNK == 0 and KEY_CHUNK % Q_BLOCK == 0 and topk % Q_BLOCK == 0
    return pl.pallas_call(
        functools.partial(_attn_kernel, topk=topk),
        grid=(bsz, nblk),
        in_specs=[pl.BlockSpec((None, Q_BLOCK, 1), lambda b, i: (b, i, 0)),
                  pl.BlockSpec((None, Q_BLOCK, ATT_WIDTH), lambda b, i: (b, i, 0))],
        out_specs=pl.BlockSpec((None, Q_BLOCK, ATT_HEADS * HEAD_DIM), lambda b, i: (b, i, 0)),
        out_shape=jax.ShapeDtypeStruct((bsz, s, ATT_HEADS * HEAD_DIM), BF16),
        scratch_shapes=[pltpu.VMEM((s, ATT_KV_HEADS * HEAD_DIM), BF16),
                        pltpu.VMEM((nblk, ATT_KV_HEADS * HEAD_DIM, Q_BLOCK), BF16),
                        pltpu.VMEM((s, LANES), BF16),
                        pltpu.VMEM((s, LANES), I32),
                        pltpu.VMEM((s, LANES), F32),
                        pltpu.VMEM((s, ATT_GROUP * Q_BLOCK), F32),
                        pltpu.VMEM((HEAD_DIM, ATT_GROUP * Q_BLOCK), F32),
                        pltpu.VMEM((1, LANES), I32)],
        compiler_params=_params(("arbitrary", "arbitrary"), 48),
        name="dsa_attention",
    )(positions.reshape(bsz, s, 1), z_att)


def _hgrn_kernel(lbp_ref, gh_ref, q_ref, f_ref, i_ref, og_ref, o_ref, *, layer, n_chunks):
    h = pl.program_id(1)
    c = HG_CHUNK
    lbp = lbp_ref[:, pl.ds(h, 1), :]
    e = jnp.exp(lbp - jnp.max(lbp, axis=0, keepdims=True))
    sm = e / jnp.sum(e, axis=0, keepdims=True)
    lb = sm[0]
    for r in range(1, layer + 1):
        lb = lb + sm[r]
    g_head = gh_ref[pl.ds(h, 1), :]

    t_io = lax.broadcasted_iota(I32, (c, c), 0)
    s_io = lax.broadcasted_iota(I32, (c, c), 1)
    tril = s_io <= t_io
    tril_f = tril.astype(F32)

    def chunk_body(n, st):
        rows = pl.ds(pl.multiple_of(n * c, c), c)
        q = _silu(q_ref[rows, :])
        f = lb + (1.0 - lb) * _sigmoid(f_ref[rows, :])
        log_f = jnp.log(f)
        k = 1.0 - f
        v = i_ref[rows, :]
        cum = jnp.dot(tril_f, log_f, preferred_element_type=F32, precision=lax.Precision.HIGHEST)
        cum_last = cum[c - 1:c, :]
        q_dec = (q * jnp.exp(cum)).astype(BF16)
        k_dec = (k * jnp.exp(-cum)).astype(BF16)
        att = lax.dot_general(q_dec, k_dec, (((1,), (1,)), ((), ())), preferred_element_type=F32)
        att = jnp.where(tril, att, 0.0)
        v16 = v.astype(BF16)
        intra = jnp.dot(att.astype(BF16), v16, preferred_element_type=F32)
        inter = lax.dot_general(q_dec, st.astype(BF16), (((1,), (1,)), ((), ())), preferred_element_type=F32)
        o = intra + inter
        y = o * lax.rsqrt(jnp.mean(o * o, axis=-1, keepdims=True) + EPS) * g_head
        o_ref[rows, :] = (y * _silu(og_ref[rows, :])).astype(o_ref.dtype)
        k_end = (k * jnp.exp(cum_last - cum)).astype(BF16)
        upd_t = jnp.dot(v.T.astype(BF16), k_end, preferred_element_type=F32)
        return jnp.exp(cum_last) * st + upd_t

    lax.fori_loop(0, n_chunks, chunk_body, jnp.zeros((HG_DIM, HG_DIM), F32), unroll=2)


def _hgrn(z_hg, lower_bounds, g_head, layer):
    bsz, s, _ = z_hg.shape
    depth1 = lower_bounds.shape[0]
    col = lambda sec: (lambda b, h: (b, 0, sec * HG_HEADS + h))
    specs = [pl.BlockSpec((None, s, HG_DIM), col(sec)) for sec in range(4)]
    return pl.pallas_call(
        functools.partial(_hgrn_kernel, layer=layer, n_chunks=s // HG_CHUNK),
        grid=(bsz, HG_HEADS),
        in_specs=[pl.BlockSpec((depth1, HG_HEADS, HG_DIM), lambda b, h: (0, 0, 0)),
                  pl.BlockSpec((HG_HEADS, HG_DIM), lambda b, h: (0, 0))] + specs,
        out_specs=pl.BlockSpec((None, s, HG_DIM), lambda b, h: (b, 0, h)),
        out_shape=jax.ShapeDtypeStruct((bsz, s, HG_HEADS * HG_DIM), BF16),
        compiler_params=_params(("arbitrary", "arbitrary"), 48),
        name="hgrn2",
    )(lower_bounds.reshape(depth1, HG_HEADS, HG_DIM), g_head, z_hg, z_hg, z_hg, z_hg)


def _merge_kernel(ya_ref, yh_ref, ga_ref, gh_ref, x_ref, mod_ref, g2_ref, wa_ref, wh_ref, wo_ref, wr_ref, br_ref,
                  x1_ref, h3_ref, cw3_ref):
    a = jnp.dot(ya_ref[...], wa_ref[...], preferred_element_type=F32)
    hg = jnp.dot(yh_ref[...], wh_ref[...], preferred_element_type=F32)
    merged = _sigmoid(ga_ref[...]) * a + _sigmoid(gh_ref[...]) * hg
    x1 = x_ref[...] + mod_ref[2:3, :] * jnp.dot(merged.astype(BF16), wo_ref[...], preferred_element_type=F32)
    x1_ref[...] = x1
    h2 = _ada_norm(x1, g2_ref[...], mod_ref[3:4, :], mod_ref[4:5, :])
    for t in range(h2.shape[1] // LANES):
        h3_ref[:, t, :] = h2[:, t * LANES:(t + 1) * LANES]

    lg = jnp.dot(h2, wr_ref[...], preferred_element_type=F32, precision=lax.Precision.HIGHEST) + br_ref[...]
    lane = lax.broadcasted_iota(I32, lg.shape, 1)
    neg = -jnp.inf
    is_grp = jnp.logical_and(lane >= N_EXPERTS, lane < N_EXPERTS + N_GROUPS)
    g_lg = jnp.where(is_grp, lg, neg)
    g_max = jnp.max(g_lg, axis=-1, keepdims=True)
    p_grp = 1.0 / jnp.sum(jnp.exp(g_lg - g_max), axis=-1, keepdims=True)
    grp = jnp.min(jnp.where(g_lg == g_max, lane, 2 * LANES), axis=-1, keepdims=True) - N_EXPERTS
    lo = grp * EXPERTS_PER_GROUP
    in_grp = jnp.logical_and(lane >= lo, lane < lo + EXPERTS_PER_GROUP)
    e_lg = jnp.where(in_grp, lg, neg)
    e_max = jnp.max(e_lg, axis=-1, keepdims=True)
    idx1 = jnp.min(jnp.where(e_lg == e_max, lane, 2 * LANES), axis=-1, keepdims=True)
    e_lg2 = jnp.where(lane == idx1, neg, e_lg)
    e_max2 = jnp.max(e_lg2, axis=-1, keepdims=True)
    idx2 = jnp.min(jnp.where(e_lg2 == e_max2, lane, 2 * LANES), axis=-1, keepdims=True)
    e2 = jnp.exp(e_max2 - e_max)
    p1 = p_grp / (1.0 + e2)
    p2 = p_grp * e2 / (1.0 + e2)
    cw = jnp.where(lane == idx1, p1, jnp.where(lane == idx2, p2, jnp.where(lane == grp + N_EXPERTS, 1.0, 0.0)))
    cw3_ref[...] = jnp.zeros_like(cw3_ref)
    cw3_ref[:, 0, :] = cw


def _merge(y_att, y_hg, z_gate, x, mod, g2, wa, wh, wo, wr, br):
    bsz, s, d = x.shape
    tm = min(256, s)
    dh = y_att.shape[-1]
    const = lambda shape: pl.BlockSpec(shape, lambda b, i: (0,) * len(shape))
    row = lambda width, j=0: pl.BlockSpec((None, tm, width), lambda b, i: (b, i, j))
    slab = lambda rows: pl.BlockSpec((None, tm, rows, LANES), lambda b, i: (b, i, 0, 0))
    return pl.pallas_call(
        _merge_kernel,
        grid=(bsz, s // tm),
        in_specs=[row(dh), row(dh), row(d, 0), row(d, 1), row(d),
                  pl.BlockSpec((None, 6, d), lambda b, i: (b, 0, 0)), const((1, d)),
                  const((dh, d)), const((dh, d)), const((d, d)), const((d, LANES)), const((1, LANES))],
        out_specs=[row(d), slab(d // LANES), slab(SUBLANES)],
        out_shape=[jax.ShapeDtypeStruct((bsz, s, d), F32),
                   jax.ShapeDtypeStruct((bsz, s, d // LANES, LANES), F32),
                   jax.ShapeDtypeStruct((bsz, s, SUBLANES, LANES), F32)],
        compiler_params=_params(("arbitrary", "arbitrary"), 56),
        name="merge_router",
    )(y_att, y_hg, z_gate, z_gate, x, mod, g2.reshape(1, d), wa, wh, wo, wr, br)


def _moe_kernel(tg_ref, nv_ref, src_ref, h3_hbm, cw3_hbm, wg_ref, wu_ref, wd_ref, y3_hbm,
                hbuf, cbuf, xb, cwv, acc, ybuf, gsem, ssem, *, tm):
    k = pl.program_id(0)
    e = pl.program_id(1)
    nt = pl.num_programs(0)
    ne = pl.num_programs(1)
    slot = k % 2
    nv = nv_ref[k]
    n_slab = xb.shape[1] // LANES

    def gather_copies(tile, slot_, r):
        tok = src_ref[tile * tm + r]
        return (pltpu.make_async_copy(h3_hbm.at[tok], hbuf.at[slot_, r], gsem.at[slot_]),
                pltpu.make_async_copy(cw3_hbm.at[tok], cbuf.at[slot_, r], gsem.at[slot_]))

    def gather(tile, slot_, start):
        def body(r, carry):
            for cp in gather_copies(tile, slot_, r):
                cp.start() if start else cp.wait()
            return carry

        lax.fori_loop(0, tm, body, 0)

    def scatter(tile, start):
        def body(r, carry):
            cp = pltpu.make_async_copy(ybuf.at[r], y3_hbm.at[src_ref[tile * tm + r]], ssem.at[0])
            cp.start() if start else cp.wait()
            return carry

        lax.fori_loop(0, nv_ref[tile], body, 0)

    @pl.when(e == 0)
    def _():
        @pl.when(jnp.logical_and(k == 0, nv > 0))
        def _():
            gather(k, slot, True)

        nxt = jnp.minimum(k + 1, nt - 1)

        @pl.when(jnp.logical_and(k + 1 < nt, nv_ref[nxt] > 0))
        def _():
            gather(nxt, 1 - slot, True)

        @pl.when(nv > 0)
        def _():
            gather(k, slot, False)
            for t in range(n_slab):
                xb[:, t * LANES:(t + 1) * LANES] = hbuf[slot, :, t, :].astype(BF16)
            cwv[...] = cbuf[slot, :, 0, :]
            acc[...] = jnp.zeros_like(acc)

    @pl.when(nv > 0)
    def _():
        h = xb[...]
        a = jnp.dot(h, wg_ref[...], preferred_element_type=F32)
        u = jnp.dot(h, wu_ref[...], preferred_element_type=F32)
        cw = cwv[...]
        lane = lax.broadcasted_iota(I32, cw.shape, 1)
        cwe = jnp.sum(jnp.where(lane == tg_ref[k] * ne + e, cw, 0.0), axis=-1, keepdims=True)
        hm = (_silu(a) * u * cwe).astype(BF16)
        acc[...] += jnp.dot(hm, wd_ref[...], preferred_element_type=F32)

    @pl.when(e == ne - 1)
    def _():
        @pl.when(k > 0)
        def _():
            scatter(jnp.maximum(k - 1, 0), False)

        @pl.when(nv > 0)
        def _():
            for t in range(n_slab):
                ybuf[:, t, :] = acc[:, t * LANES:(t + 1) * LANES]
            scatter(k, True)

        @pl.when(k == nt - 1)
        def _():
            scatter(k, False)


def _moe_tiles(grp, tm):
    t = grp.shape[0]
    nt = t // tm + N_GROUPS
    order = jnp.argsort(grp, stable=True).astype(I32)
    counts = jnp.sum((grp[:, None] == jnp.arange(N_GROUPS, dtype=I32)[None, :]).astype(I32), axis=0)
    tiles_g = (counts + tm - 1) // tm
    tile_end = jnp.cumsum(tiles_g)
    tile0 = tile_end - tiles_g
    gstart = jnp.cumsum(counts) - counts
    k = jnp.arange(nt, dtype=I32)
    used = k < tile_end[-1]
    tg = jnp.sum((k[:, None] >= tile_end[None, :]).astype(I32), axis=1)
    last_g = jnp.sum((tile_end[-1] - 1 >= tile_end).astype(I32))
    tg = jnp.where(used, tg, last_g)
    q0 = (k - tile0[tg]) * tm
    nvalid = jnp.where(used, jnp.clip(counts[tg] - q0, 0, tm), 0).astype(I32)
    r = jnp.arange(tm, dtype=I32)
    idx = jnp.clip(gstart[tg][:, None] + q0[:, None] + r[None, :], 0, t - 1)
    src = jnp.where(r[None, :] < nvalid[:, None], order[idx], 0).astype(I32)
    return tg.astype(I32), nvalid, src.reshape(nt * tm)


def _moe(h3, cw3, wg, wu, wd):
    t, n_slab, _ = h3.shape
    d = n_slab * LANES
    ne_all, _, ff = wg.shape
    tm = min(512, t)
    grp = jnp.argmax(cw3[:, 0, N_EXPERTS:N_EXPERTS + N_GROUPS], axis=-1).astype(I32)
    tg, nvalid, src = _moe_tiles(grp, tm)
    nt = tg.shape[0]
    w_idx = lambda k, e, tg_r, nv_r, src_r: (tg_r[k] * EXPERTS_PER_GROUP + jnp.where(nv_r[k] > 0, e, EXPERTS_PER_GROUP - 1),
                                           0, 0)
    grid_spec = pltpu.PrefetchScalarGridSpec(
        num_scalar_prefetch=3,
        grid=(nt, EXPERTS_PER_GROUP),
        in_specs=[pl.BlockSpec(memory_space=pl.ANY), pl.BlockSpec(memory_space=pl.ANY),
                  pl.BlockSpec((None, d, ff), w_idx), pl.BlockSpec((None, d, ff), w_idx),
                  pl.BlockSpec((None, ff, d), w_idx)],
        out_specs=pl.BlockSpec(memory_space=pl.ANY),
        scratch_shapes=[pltpu.VMEM((2, tm, n_slab, LANES), F32),
                        pltpu.VMEM((2, tm, SUBLANES, LANES), F32),
                        pltpu.VMEM((tm, d), BF16),
                        pltpu.VMEM((tm, LANES), F32),
                        pltpu.VMEM((tm, d), F32),
                        pltpu.VMEM((tm, n_slab, LANES), F32),
                        pltpu.SemaphoreType.DMA((2,)),
                        pltpu.SemaphoreType.DMA((1,))])
    return pl.pallas_call(
        functools.partial(_moe_kernel, tm=tm),
        grid_spec=grid_spec,
        out_shape=jax.ShapeDtypeStruct((t, n_slab, LANES), F32),
        compiler_params=_params(("arbitrary", "arbitrary"), 56),
        name="moe_experts",
    )(tg, nvalid, src, h3, cw3, wg, wu, wd)


def _final_kernel(x1_ref, y3_ref, mod_ref, g_ref, o_ref, *, gate_row):
    y = jnp.concatenate([y3_ref[:, t, :] for t in range(y3_ref.shape[1])], axis=1)
    x2 = x1_ref[...] + mod_ref[gate_row:gate_row + 1, :] * y
    ms = jnp.mean(x2 * x2, axis=-1, keepdims=True)
    o_ref[...] = x2 * lax.rsqrt(ms + EPS) * g_ref[...]


def _final(x1, y3, mod, g):
    bsz, s, d = x1.shape
    tm = min(512, s)
    return pl.pallas_call(
        functools.partial(_final_kernel, gate_row=5),
        grid=(bsz, s // tm),
        in_specs=[pl.BlockSpec((None, tm, d), lambda b, i: (b, i, 0)),
                  pl.BlockSpec((None, tm, d // LANES, LANES), lambda b, i: (b, i, 0, 0)),
                  pl.BlockSpec((None, 6, d), lambda b, i: (b, 0, 0)),
                  pl.BlockSpec((1, d), lambda b, i: (0, 0))],
        out_specs=pl.BlockSpec((None, tm, d), lambda b, i: (b, i, 0)),
        out_shape=jax.ShapeDtypeStruct((bsz, s, d), F32),
        compiler_params=_params(("arbitrary", "arbitrary"), 40),
        name="residual_final_norm",
    )(x1, y3, mod, g.reshape(1, d))


def kernel(x, c, positions, w_ada, b_ada, g_norm1, w_in, g_head, hg_lower_bounds, w_attn_up, w_hgrn_up, w_out,
           g_norm2, w_router_group, b_router_group, w_router_expert, b_router_expert, w_exp_gate, w_exp_up,
           w_exp_down, g_final):
    depth = w_ada.shape[0]
    d = x.shape[-1]
    assert depth == 1
    hg_w = HG_HEADS * HG_DIM
    for l in range(depth):
        mod = _modulation(c, w_ada[l], b_ada[l])
        w_l = w_in[l]
        w_att = jnp.pad(w_l[:, :ATT_COLS], ((0, 0), (0, ATT_WIDTH - ATT_COLS))).astype(BF16)
        w_hg = w_l[:, ATT_COLS:ATT_COLS + 4 * hg_w].astype(BF16)
        w_gate = w_l[:, ATT_COLS + 4 * hg_w:].astype(BF16)
        z_att = _norm_matmul(x, mod, g_norm1[l], w_att, ATT_WIDTH)
        z_hg = _norm_matmul(x, mod, g_norm1[l], w_hg, 1024)
        z_gate = _norm_matmul(x, mod, g_norm1[l], w_gate, 1024)
        y_att = _attention(z_att, positions)
        y_hg = _hgrn(z_hg, hg_lower_bounds, g_head[l], l)
        w_r = jnp.pad(jnp.concatenate([w_router_expert[l], w_router_group[l]], axis=1),
                      ((0, 0), (0, LANES - N_EXPERTS - N_GROUPS)))
        b_r = jnp.pad(jnp.concatenate([b_router_expert[l], b_router_group[l]]),
                      (0, LANES - N_EXPERTS - N_GROUPS)).reshape(1, LANES)
        x1, h3, cw3 = _merge(y_att, y_hg, z_gate, x, mod, g_norm2[l],
                             w_attn_up[l].astype(BF16), w_hgrn_up[l].astype(BF16), w_out[l].astype(BF16), w_r, b_r)
        bsz, s = x.shape[:2]
        y3 = _moe(h3.reshape(bsz * s, d // LANES, LANES), cw3.reshape(bsz * s, SUBLANES, LANES),
                  w_exp_gate[l].astype(BF16), w_exp_up[l].astype(BF16), w_exp_down[l].astype(BF16))
        x = _final(x1, y3.reshape(bsz, s, d // LANES, LANES), mod, g_final)
    return x
```

```python
import functools
import math

import jax
import jax.numpy as jnp
from jax import lax
from jax.experimental import pallas as pl
from jax.experimental.pallas import tpu as pltpu

F32 = jnp.float32
BF16 = jnp.bfloat16
I32 = jnp.int32

ATT_HEADS = 8
ATT_KV_HEADS = 2
ATT_GROUP = ATT_HEADS // ATT_KV_HEADS
HEAD_DIM = 128
IDX_HEADS = 8
IDX_DIM = 64
TOPK_MAX = 256
Q_BLOCK = 128
KEY_CHUNK = 512
ROPE_THETA = 10000.0
ATT_SCALE = HEAD_DIM ** -0.5
IDX_SCALE = (IDX_HEADS * IDX_DIM) ** -0.5
HG_HEADS = 8
HG_DIM = 128
HG_CHUNK = 64
N_GROUPS = 4
EXPERTS_PER_GROUP = 8
N_EXPERTS = N_GROUPS * EXPERTS_PER_GROUP
EPS = 1e-6

LANES = 128
SUBLANES = 8
INT_MIN = -(2 ** 31)
NEG_INF_KEY = INT_MIN + 0x7FFFFF
MIB = 1024 * 1024

AQ_OFF = 0
AK_OFF = ATT_HEADS * HEAD_DIM
AV_OFF = AK_OFF + ATT_KV_HEADS * HEAD_DIM
IQ_OFF = AV_OFF + ATT_KV_HEADS * HEAD_DIM
IK_OFF = IQ_OFF + IDX_HEADS * IDX_DIM
ATT_COLS = IK_OFF + IDX_DIM + IDX_HEADS
ATT_WIDTH = IK_OFF + LANES


def _sigmoid(x):
    return 1.0 / (1.0 + jnp.exp(-x))


def _silu(x):
    return x * _sigmoid(x)


def _params(sem, vmem_mib):
    return pltpu.CompilerParams(dimension_semantics=sem, vmem_limit_bytes=vmem_mib * MIB)


def _mod_kernel(c_ref, w_ref, b_ref, o_ref):
    c = c_ref[...]
    ca = _silu(c).astype(BF16)
    o_ref[...] = jnp.dot(ca, w_ref[...].astype(BF16), preferred_element_type=F32) + b_ref[...]


def _modulation(c, w, b):
    bsz, d = c.shape
    n = w.shape[1]
    rows = 8
    cp = jnp.zeros((rows, d), F32).at[:bsz].set(c)
    tn = 1024
    out = pl.pallas_call(
        _mod_kernel,
        grid=(n // tn,),
        in_specs=[pl.BlockSpec((rows, d), lambda j: (0, 0)),
                  pl.BlockSpec((d, tn), lambda j: (0, j)),
                  pl.BlockSpec((1, tn), lambda j: (0, j))],
        out_specs=pl.BlockSpec((rows, tn), lambda j: (0, j)),
        out_shape=jax.ShapeDtypeStruct((rows, n), F32),
        compiler_params=_params(("arbitrary",), 40),
        name="modulation",
    )(cp, w, b.reshape(1, n))
    return out[:bsz].reshape(bsz, 6, d)


def _ada_norm(x, g, shift, scale):
    ms = jnp.mean(x * x, axis=-1, keepdims=True)
    y = x * lax.rsqrt(ms + EPS) * g
    return y * (1.0 + scale) + shift


def _norm_matmul_kernel(x_ref, mod_ref, g_ref, w_ref, o_ref, h_ref):
    @pl.when(pl.program_id(2) == 0)
    def _():
        h = _ada_norm(x_ref[...], g_ref[...], mod_ref[0:1, :], mod_ref[1:2, :])
        h_ref[...] = h.astype(BF16)

    o_ref[...] = jnp.dot(h_ref[...], w_ref[...], preferred_element_type=F32)


def _norm_matmul(x, mod, g, w, tn):
    bsz, s, d = x.shape
    n = w.shape[1]
    tm = min(1024 if tn <= 1024 else 512, s)
    return pl.pallas_call(
        _norm_matmul_kernel,
        grid=(bsz, s // tm, n // tn),
        in_specs=[pl.BlockSpec((None, tm, d), lambda b, i, j: (b, i, 0)),
                  pl.BlockSpec((None, 6, d), lambda b, i, j: (b, 0, 0)),
                  pl.BlockSpec((1, d), lambda b, i, j: (0, 0)),
                  pl.BlockSpec((d, tn), lambda b, i, j: (0, j))],
        out_specs=pl.BlockSpec((None, tm, tn), lambda b, i, j: (b, i, j)),
        out_shape=jax.ShapeDtypeStruct((bsz, s, n), F32),
        scratch_shapes=[pltpu.VMEM((tm, d), BF16)],
        compiler_params=_params(("arbitrary", "arbitrary", "arbitrary"), 48),
        name="norm_matmul",
    )(x, mod, g.reshape(1, d), w)


def _sortable(score):
    bits = pltpu.bitcast(score + 0.0, I32)
    return jnp.where(bits < 0, bits ^ 0x7FFFFFFF, bits)


def _attn_kernel(pos_ref, z_ref, o_ref, k_s, vt_s, ik_s, key_s, bias_s, lg_s, acc_s, thr_s, *, topk):
    i = pl.program_id(1)
    qb = Q_BLOCK
    kc = KEY_CHUNK
    n_chunks = (i * qb + kc) // kc
    r0 = pl.multiple_of(i * qb, qb)

    @pl.when(jnp.logical_and(pl.program_id(0) == 0, i == 0))
    def _():
        k_s[...] = jnp.zeros_like(k_s)
        vt_s[...] = jnp.zeros_like(vt_s)
        ik_s[...] = jnp.zeros_like(ik_s)

    lane = lax.broadcasted_iota(I32, (1, LANES), 1)
    pos = pos_ref[...].astype(F32)
    inv_a = jnp.exp((lane % (HEAD_DIM // 2)).astype(F32) * (-2.0 * math.log(ROPE_THETA) / HEAD_DIM))
    ang_a = pos * inv_a
    cos_a = jnp.cos(ang_a)
    sin_a = jnp.where(lane < HEAD_DIM // 2, -1.0, 1.0) * jnp.sin(ang_a)
    inv_i = jnp.exp((lane % (IDX_DIM // 2)).astype(F32) * (-2.0 * math.log(ROPE_THETA) / IDX_DIM))
    ang_i = pos * inv_i
    low_i = (lane % IDX_DIM) < IDX_DIM // 2
    cos_i = jnp.cos(ang_i)
    sin_i = jnp.where(low_i, -1.0, 1.0) * jnp.sin(ang_i)

    def rope_a(x):
        return x * cos_a + pltpu.roll(x, HEAD_DIM // 2, 1) * sin_a

    def rope_i(x):
        partner = jnp.where(low_i, pltpu.roll(x, LANES - IDX_DIM // 2, 1), pltpu.roll(x, IDX_DIM // 2, 1))
        return x * cos_i + partner * sin_i

    for h in range(ATT_KV_HEADS):
        kh = rope_a(z_ref[:, AK_OFF + h * HEAD_DIM:AK_OFF + (h + 1) * HEAD_DIM])
        k_s[pl.ds(r0, qb), h * HEAD_DIM:(h + 1) * HEAD_DIM] = kh.astype(BF16)
    v = z_ref[:, AV_OFF:AV_OFF + ATT_KV_HEADS * HEAD_DIM]
    vt_s[i] = v.T.astype(BF16)
    tile = z_ref[:, IK_OFF:IK_OFF + LANES]
    ikr = rope_i(tile)
    ik2 = jnp.where(lane < IDX_DIM, ikr, pltpu.roll(ikr, IDX_DIM, 1))
    ik_s[pl.ds(r0, qb), :] = ik2.astype(BF16)
    w_t = tile.T[IDX_DIM:IDX_DIM + IDX_HEADS, :] * IDX_SCALE

    q_rows = []
    for t in range(IDX_HEADS // 2):
        iqr = rope_i(z_ref[:, IQ_OFF + t * LANES:IQ_OFF + (t + 1) * LANES])
        q_rows.append(jnp.where(lane < IDX_DIM, iqr, 0.0).astype(BF16))
        q_rows.append(jnp.where(lane < IDX_DIM, 0.0, iqr).astype(BF16))
    iq_all = jnp.concatenate(q_rows, axis=0)

    key_iota = lax.broadcasted_iota(I32, (kc, LANES), 0)
    q_idx = i * qb + lane

    def chunk_rows(j):
        return pl.ds(pl.multiple_of(j * kc, kc), kc)

    zero8 = jnp.zeros((SUBLANES, LANES), I32)

    def count8(mask):
        return jnp.sum(jnp.where(mask, 1, 0).reshape(kc // SUBLANES, SUBLANES, LANES), axis=0)

    def score_body(j, carry):
        dots = lax.dot_general(ik_s[chunk_rows(j), :], iq_all, (((1,), (1,)), ((), ())),
                               preferred_element_type=F32)
        sc = jnp.zeros((kc, LANES), F32)
        for h in range(IDX_HEADS):
            sc = sc + jnp.maximum(dots[:, h * qb:(h + 1) * qb], 0.0) * w_t[h:h + 1, :]
        sc = jnp.where(key_iota + j * kc <= q_idx, sc, -jnp.inf)
        key_s[chunk_rows(j), :] = _sortable(sc)
        return carry

    lax.fori_loop(0, n_chunks, score_body, 0)

    thr_s[...] = jnp.full((1, LANES), NEG_INF_KEY + 1, I32)
    select = i >= topk // qb

    @pl.when(select)
    def _():
        def bit_body(b, t_u):
            cand = t_u | jnp.left_shift(jnp.int32(1), 31 - b)
            cand_s = cand ^ INT_MIN

            def cnt_body(j, acc):
                return acc + count8(key_s[chunk_rows(j), :] >= cand_s)

            cnt = jnp.sum(lax.fori_loop(0, n_chunks, cnt_body, zero8), axis=0, keepdims=True)
            return jnp.where(cnt >= topk, cand, t_u)

        t_u = lax.fori_loop(0, 32, bit_body, jnp.zeros((1, LANES), I32))
        thr_s[...] = t_u ^ INT_MIN

    thr = thr_s[...]

    def bias_body(j, cnt):
        sel = key_s[chunk_rows(j), :] >= thr
        bias_s[chunk_rows(j), :] = jnp.where(sel, 0.0, -jnp.inf)
        return cnt + count8(sel)

    cnt_ge = jnp.sum(lax.fori_loop(0, n_chunks, bias_body, zero8), axis=0, keepdims=True)
    has_ties = jnp.logical_and(select, jnp.max(cnt_ge) > topk)

    @pl.when(has_ties)
    def _():
        def eq_body(j, acc):
            return acc + count8(key_s[chunk_rows(j), :] == thr)

        cnt_eq = jnp.sum(lax.fori_loop(0, n_chunks, eq_body, zero8), axis=0, keepdims=True)
        n_keep = (topk - (cnt_ge - cnt_eq)).astype(F32)
        tril = (lax.broadcasted_iota(I32, (kc, kc), 1) <= lax.broadcasted_iota(I32, (kc, kc), 0)).astype(BF16)

        def tie_body(j, run):
            kk = key_s[chunk_rows(j), :]
            eq = kk == thr
            pref = jnp.dot(tril, jnp.where(eq, 1.0, 0.0).astype(BF16), preferred_element_type=F32) + run
            sel = jnp.logical_or(kk > thr, jnp.logical_and(eq, pref <= n_keep))
            bias_s[chunk_rows(j), :] = jnp.where(sel, 0.0, -jnp.inf)
            return pref[kc - 1:kc, :]

        lax.fori_loop(0, n_chunks, tie_body, jnp.zeros((1, LANES), F32))

    gw = ATT_GROUP * qb
    for h in range(ATT_KV_HEADS):
        qs = jnp.concatenate(
            [(rope_a(z_ref[:, AQ_OFF + (h * ATT_GROUP + g) * HEAD_DIM:AQ_OFF + (h * ATT_GROUP + g + 1) * HEAD_DIM])
              * ATT_SCALE).astype(BF16) for g in range(ATT_GROUP)], axis=0)

        def logit_body(j, m8):
            kch = k_s[chunk_rows(j), h * HEAD_DIM:(h + 1) * HEAD_DIM]
            lt = lax.dot_general(kch, qs, (((1,), (1,)), ((), ())), preferred_element_type=F32)
            bias = bias_s[chunk_rows(j), :]
            lm = lt + jnp.concatenate([bias] * ATT_GROUP, axis=1)
            lg_s[chunk_rows(j), :] = lm
            return jnp.maximum(m8, jnp.max(lm.reshape(kc // 8, 8, gw), axis=0))

        m8 = lax.fori_loop(0, n_chunks, logit_body, jnp.full((8, gw), -jnp.inf, F32))
        m = jnp.max(m8, axis=0, keepdims=True)
        acc_s[...] = jnp.zeros_like(acc_s)

        def pv_body(j, l8):
            p = jnp.exp(lg_s[chunk_rows(j), :] - m)
            vt = jnp.concatenate([vt_s[(kc // qb) * j + t][h * HEAD_DIM:(h + 1) * HEAD_DIM, :]
                                  for t in range(kc // qb)], axis=1)
            acc_s[...] += jnp.dot(vt, p.astype(BF16), preferred_element_type=F32)
            return l8 + jnp.sum(p.reshape(kc // 8, 8, gw), axis=0)

        l8 = lax.fori_loop(0, n_chunks, pv_body, jnp.zeros((8, gw), F32))
        o_t = acc_s[...] / jnp.sum(l8, axis=0, keepdims=True)
        for g in range(ATT_GROUP):
            c0 = (h * ATT_GROUP + g) * HEAD_DIM
            o_ref[:, c0:c0 + HEAD_DIM] = o_t[:, g * qb:(g + 1) * qb].T.astype(o_ref.dtype)


def _attention(z_att, positions):
    bsz, s, _ = z_att.shape
    nblk = s // Q_BLOCK
    topk = min(TOPK_MAX, s // 4)
    assert s % KEY_CHUNK == 0 and KEY_CHUNK % Q_BLOCK == 0 and topk % Q_BLOCK == 0
    return pl.pallas_call(
        functools.partial(_attn_kernel, topk=topk),
        grid=(bsz, nblk),
        in_specs=[pl.BlockSpec((None, Q_BLOCK, 1), lambda b, i: (b, i, 0)),
                  pl.BlockSpec((None, Q_BLOCK, ATT_WIDTH), lambda b, i: (b, i, 0))],
        out_specs=pl.BlockSpec((None, Q_BLOCK, ATT_HEADS * HEAD_DIM), lambda b, i: (b, i, 0)),
        out_shape=jax.ShapeDtypeStruct((bsz, s, ATT_HEADS * HEAD_DIM), BF16),
        scratch_shapes=[pltpu.VMEM((s, ATT_KV_HEADS * HEAD_DIM), BF16),
                        pltpu.VMEM((nblk, ATT_KV_HEADS * HEAD_DIM, Q_BLOCK), BF16),
                        pltpu.VMEM((s, LANES), BF16),
                        pltpu.VMEM((s, LANES), I32),
                        pltpu.VMEM((s, LANES), F32),
                        pltpu.VMEM((s, ATT_GROUP * Q_BLOCK), F32),
                        pltpu.VMEM((HEAD_DIM, ATT_GROUP * Q_BLOCK), F32),
                        pltpu.VMEM((1, LANES), I32)],
        compiler_params=_params(("arbitrary", "arbitrary"), 48),
        name="dsa_attention",
    )(positions.reshape(bsz, s, 1), z_att)


def _hgrn_kernel(lbp_ref, gh_ref, q_ref, f_ref, i_ref, og_ref, o_ref, *, layer, n_chunks):
    h = pl.program_id(1)
    c = HG_CHUNK
    lbp = lbp_ref[:, pl.ds(h, 1), :]
    e = jnp.exp(lbp - jnp.max(lbp, axis=0, keepdims=True))
    sm = e / jnp.sum(e, axis=0, keepdims=True)
    lb = sm[0]
    for r in range(1, layer + 1):
        lb = lb + sm[r]
    g_head = gh_ref[pl.ds(h, 1), :]

    t_io = lax.broadcasted_iota(I32, (c, c), 0)
    s_io = lax.broadcasted_iota(I32, (c, c), 1)
    tril = s_io <= t_io
    tril_f = tril.astype(F32)

    def chunk_body(n, st):
        rows = pl.ds(pl.multiple_of(n * c, c), c)
        q = _silu(q_ref[rows, :])
        f = lb + (1.0 - lb) * _sigmoid(f_ref[rows, :])
        log_f = jnp.log(f)
        k = 1.0 - f
        v = i_ref[rows, :]
        cum = jnp.dot(tril_f, log_f, preferred_element_type=F32, precision=lax.Precision.HIGHEST)
        cum_last = cum[c - 1:c, :]
        q_dec = (q * jnp.exp(cum)).astype(BF16)
        k_dec = (k * jnp.exp(-cum)).astype(BF16)
        att = lax.dot_general(q_dec, k_dec, (((1,), (1,)), ((), ())), preferred_element_type=F32)
        att = jnp.where(tril, att, 0.0)
        v16 = v.astype(BF16)
        intra = jnp.dot(att.astype(BF16), v16, preferred_element_type=F32)
        inter = lax.dot_general(q_dec, st.astype(BF16), (((1,), (1,)), ((), ())), preferred_element_type=F32)
        o = intra + inter
        y = o * lax.rsqrt(jnp.mean(o * o, axis=-1, keepdims=True) + EPS) * g_head
        o_ref[rows, :] = (y * _silu(og_ref[rows, :])).astype(o_ref.dtype)
        k_end = (k * jnp.exp(cum_last - cum)).astype(BF16)
        upd_t = jnp.dot(v.T.astype(BF16), k_end, preferred_element_type=F32)
        return jnp.exp(cum_last) * st + upd_t

    lax.fori_loop(0, n_chunks, chunk_body, jnp.zeros((HG_DIM, HG_DIM), F32), unroll=4)


def _hgrn(z_hg, lower_bounds, g_head, layer):
    bsz, s, _ = z_hg.shape
    depth1 = lower_bounds.shape[0]
    col = lambda sec: (lambda b, h: (b, 0, sec * HG_HEADS + h))
    specs = [pl.BlockSpec((None, s, HG_DIM), col(sec)) for sec in range(4)]
    return pl.pallas_call(
        functools.partial(_hgrn_kernel, layer=layer, n_chunks=s // HG_CHUNK),
        grid=(bsz, HG_HEADS),
        in_specs=[pl.BlockSpec((depth1, HG_HEADS, HG_DIM), lambda b, h: (0, 0, 0)),
                  pl.BlockSpec((HG_HEADS, HG_DIM), lambda b, h: (0, 0))] + specs,
        out_specs=pl.BlockSpec((None, s, HG_DIM), lambda b, h: (b, 0, h)),
        out_shape=jax.ShapeDtypeStruct((bsz, s, HG_HEADS * HG_DIM), BF16),
        compiler_params=_params(("arbitrary", "arbitrary"), 48),
        name="hgrn2",
    )(lower_bounds.reshape(depth1, HG_HEADS, HG_DIM), g_head, z_hg, z_hg, z_hg, z_hg)


def _merge_kernel(ya_ref, yh_ref, ga_ref, gh_ref, x_ref, mod_ref, g2_ref, wa_ref, wh_ref, wo_ref, wr_ref, br_ref,
                  x1_ref, h3_ref, cw3_ref):
    a = jnp.dot(ya_ref[...], wa_ref[...], preferred_element_type=F32)
    hg = jnp.dot(yh_ref[...], wh_ref[...], preferred_element_type=F32)
    merged = _sigmoid(ga_ref[...]) * a + _sigmoid(gh_ref[...]) * hg
    x1 = x_ref[...] + mod_ref[2:3, :] * jnp.dot(merged.astype(BF16), wo_ref[...], preferred_element_type=F32)
    x1_ref[...] = x1
    h2 = _ada_norm(x1, g2_ref[...], mod_ref[3:4, :], mod_ref[4:5, :])
    for t in range(h2.shape[1] // LANES):
        h3_ref[:, t, :] = h2[:, t * LANES:(t + 1) * LANES]

    lg = jnp.dot(h2, wr_ref[...], preferred_element_type=F32, precision=lax.Precision.HIGHEST) + br_ref[...]
    lane = lax.broadcasted_iota(I32, lg.shape, 1)
    neg = -jnp.inf
    is_grp = jnp.logical_and(lane >= N_EXPERTS, lane < N_EXPERTS + N_GROUPS)
    g_lg = jnp.where(is_grp, lg, neg)
    g_max = jnp.max(g_lg, axis=-1, keepdims=True)
    p_grp = 1.0 / jnp.sum(jnp.exp(g_lg - g_max), axis=-1, keepdims=True)
    grp = jnp.min(jnp.where(g_lg == g_max, lane, 2 * LANES), axis=-1, keepdims=True) - N_EXPERTS
    lo = grp * EXPERTS_PER_GROUP
    in_grp = jnp.logical_and(lane >= lo, lane < lo + EXPERTS_PER_GROUP)
    e_lg = jnp.where(in_grp, lg, neg)
    e_max = jnp.max(e_lg, axis=-1, keepdims=True)
    idx1 = jnp.min(jnp.where(e_lg == e_max, lane, 2 * LANES), axis=-1, keepdims=True)
    e_lg2 = jnp.where(lane == idx1, neg, e_lg)
    e_max2 = jnp.max(e_lg2, axis=-1, keepdims=True)
    idx2 = jnp.min(jnp.where(e_lg2 == e_max2, lane, 2 * LANES), axis=-1, keepdims=True)
    e2 = jnp.exp(e_max2 - e_max)
    p1 = p_grp / (1.0 + e2)
    p2 = p_grp * e2 / (1.0 + e2)
    cw = jnp.where(lane == idx1, p1, jnp.where(lane == idx2, p2, jnp.where(lane == grp + N_EXPERTS, 1.0, 0.0)))
    cw3_ref[...] = jnp.zeros_like(cw3_ref)
    cw3_ref[:, 0, :] = cw


def _merge(y_att, y_hg, z_gate, x, mod, g2, wa, wh, wo, wr, br):
    bsz, s, d = x.shape
    tm = min(256, s)
    dh = y_att.shape[-1]
    const = lambda shape: pl.BlockSpec(shape, lambda b, i: (0,) * len(shape))
    row = lambda width, j=0: pl.BlockSpec((None, tm, width), lambda b, i: (b, i, j))
    slab = lambda rows: pl.BlockSpec((None, tm, rows, LANES), lambda b, i: (b, i, 0, 0))
    return pl.pallas_call(
        _merge_kernel,
        grid=(bsz, s // tm),
        in_specs=[row(dh), row(dh), row(d, 0), row(d, 1), row(d),
                  pl.BlockSpec((None, 6, d), lambda b, i: (b, 0, 0)), const((1, d)),
                  const((dh, d)), const((dh, d)), const((d, d)), const((d, LANES)), const((1, LANES))],
        out_specs=[row(d), slab(d // LANES), slab(SUBLANES)],
        out_shape=[jax.ShapeDtypeStruct((bsz, s, d), F32),
                   jax.ShapeDtypeStruct((bsz, s, d // LANES, LANES), F32),
                   jax.ShapeDtypeStruct((bsz, s, SUBLANES, LANES), F32)],
        compiler_params=_params(("arbitrary", "arbitrary"), 56),
        name="merge_router",
    )(y_att, y_hg, z_gate, z_gate, x, mod, g2.reshape(1, d), wa, wh, wo, wr, br)


def _moe_kernel(tg_ref, nv_ref, src_ref, h3_hbm, cw3_hbm, wg_ref, wu_ref, wd_ref, y3_hbm,
                hbuf, cbuf, xb, cwv, acc, ybuf, gsem, ssem, *, tm):
    k = pl.program_id(0)
    e = pl.program_id(1)
    nt = pl.num_programs(0)
    ne = pl.num_programs(1)
    slot = k % 2
    nv = nv_ref[k]
    n_slab = xb.shape[1] // LANES

    def gather_copies(tile, slot_, r):
        tok = src_ref[tile * tm + r]
        return (pltpu.make_async_copy(h3_hbm.at[tok], hbuf.at[slot_, r], gsem.at[slot_]),
                pltpu.make_async_copy(cw3_hbm.at[tok], cbuf.at[slot_, r], gsem.at[slot_]))

    def gather(tile, slot_, start):
        def body(r, carry):
            for cp in gather_copies(tile, slot_, r):
                cp.start() if start else cp.wait()
            return carry

        lax.fori_loop(0, tm, body, 0, unroll=8)

    def scatter(tile, start):
        def body(r, carry):
            cp = pltpu.make_async_copy(ybuf.at[r], y3_hbm.at[src_ref[tile * tm + r]], ssem.at[0])
            cp.start() if start else cp.wait()
            return carry

        lax.fori_loop(0, nv_ref[tile], body, 0)

    @pl.when(e == 0)
    def _():
        @pl.when(jnp.logical_and(k == 0, nv > 0))
        def _():
            gather(k, slot, True)

        nxt = jnp.minimum(k + 1, nt - 1)

        @pl.when(jnp.logical_and(k + 1 < nt, nv_ref[nxt] > 0))
        def _():
            gather(nxt, 1 - slot, True)

        @pl.when(nv > 0)
        def _():
            gather(k, slot, False)
            for t in range(n_slab):
                xb[:, t * LANES:(t + 1) * LANES] = hbuf[slot, :, t, :].astype(BF16)
            cwv[...] = cbuf[slot, :, 0, :]
            acc[...] = jnp.zeros_like(acc)

    @pl.when(nv > 0)
    def _():
        h = xb[...]
        a = jnp.dot(h, wg_ref[...], preferred_element_type=F32)
        u = jnp.dot(h, wu_ref[...], preferred_element_type=F32)
        cw = cwv[...]
        lane = lax.broadcasted_iota(I32, cw.shape, 1)
        cwe = jnp.sum(jnp.where(lane == tg_ref[k] * ne + e, cw, 0.0), axis=-1, keepdims=True)
        hm = (_silu(a) * u * cwe).astype(BF16)
        acc[...] += jnp.dot(hm, wd_ref[...], preferred_element_type=F32)

    @pl.when(e == ne - 1)
    def _():
        @pl.when(k > 0)
        def _():
            scatter(jnp.maximum(k - 1, 0), False)

        @pl.when(nv > 0)
        def _():
            for t in range(n_slab):
                ybuf[:, t, :] = acc[:, t * LANES:(t + 1) * LANES]
            scatter(k, True)

        @pl.when(k == nt - 1)
        def _():
            scatter(k, False)


def _moe_tiles(grp, tm):
    t = grp.shape[0]
    nt = t // tm + N_GROUPS
    order = jnp.argsort(grp, stable=True).astype(I32)
    counts = jnp.sum((grp[:, None] == jnp.arange(N_GROUPS, dtype=I32)[None, :]).astype(I32), axis=0)
    tiles_g = (counts + tm - 1) // tm
    tile_end = jnp.cumsum(tiles_g)
    tile0 = tile_end - tiles_g
    gstart = jnp.cumsum(counts) - counts
    k = jnp.arange(nt, dtype=I32)
    used = k < tile_end[-1]
    tg = jnp.sum((k[:, None] >= tile_end[None, :]).astype(I32), axis=1)
    last_g = jnp.sum((tile_end[-1] - 1 >= tile_end).astype(I32))
    tg = jnp.where(used, tg, last_g)
    q0 = (k - tile0[tg]) * tm
    nvalid = jnp.where(used, jnp.clip(counts[tg] - q0, 0, tm), 0).astype(I32)
    r = jnp.arange(tm, dtype=I32)
    idx = jnp.clip(gstart[tg][:, None] + q0[:, None] + r[None, :], 0, t - 1)
    src = jnp.where(r[None, :] < nvalid[:, None], order[idx], 0).astype(I32)
    return tg.astype(I32), nvalid, src.reshape(nt * tm)


def _moe(h3, cw3, wg, wu, wd):
    t, n_slab, _ = h3.shape
    d = n_slab * LANES
    ne_all, _, ff = wg.shape
    tm = min(512, t)
    grp = jnp.argmax(cw3[:, 0, N_EXPERTS:N_EXPERTS + N_GROUPS], axis=-1).astype(I32)
    tg, nvalid, src = _moe_tiles(grp, tm)
    nt = tg.shape[0]
    w_idx = lambda k, e, tg_r, nv_r, src_r: (tg_r[k] * EXPERTS_PER_GROUP + jnp.where(nv_r[k] > 0, e, EXPERTS_PER_GROUP - 1),
                                           0, 0)
    grid_spec = pltpu.PrefetchScalarGridSpec(
        num_scalar_prefetch=3,
        grid=(nt, EXPERTS_PER_GROUP),
        in_specs=[pl.BlockSpec(memory_space=pl.ANY), pl.BlockSpec(memory_space=pl.ANY),
                  pl.BlockSpec((None, d, ff), w_idx), pl.BlockSpec((None, d, ff), w_idx),
                  pl.BlockSpec((None, ff, d), w_idx)],
        out_specs=pl.BlockSpec(memory_space=pl.ANY),
        scratch_shapes=[pltpu.VMEM((2, tm, n_slab, LANES), F32),
                        pltpu.VMEM((2, tm, SUBLANES, LANES), F32),
                        pltpu.VMEM((tm, d), BF16),
                        pltpu.VMEM((tm, LANES), F32),
                        pltpu.VMEM((tm, d), F32),
                        pltpu.VMEM((tm, n_slab, LANES), F32),
                        pltpu.SemaphoreType.DMA((2,)),
                        pltpu.SemaphoreType.DMA((1,))])
    return pl.pallas_call(
        functools.partial(_moe_kernel, tm=tm),
        grid_spec=grid_spec,
        out_shape=jax.ShapeDtypeStruct((t, n_slab, LANES), F32),
        compiler_params=_params(("arbitrary", "arbitrary"), 56),
        name="moe_experts",
    )(tg, nvalid, src, h3, cw3, wg, wu, wd)


def _final_kernel(x1_ref, y3_ref, mod_ref, g_ref, o_ref, *, gate_row):
    y = jnp.concatenate([y3_ref[:, t, :] for t in range(y3_ref.shape[1])], axis=1)
    x2 = x1_ref[...] + mod_ref[gate_row:gate_row + 1, :] * y
    ms = jnp.mean(x2 * x2, axis=-1, keepdims=True)
    o_ref[...] = x2 * lax.rsqrt(ms + EPS) * g_ref[...]


def _final(x1, y3, mod, g):
    bsz, s, d = x1.shape
    tm = min(512, s)
    return pl.pallas_call(
        functools.partial(_final_kernel, gate_row=5),
        grid=(bsz, s // tm),
        in_specs=[pl.BlockSpec((None, tm, d), lambda b, i: (b, i, 0)),
                  pl.BlockSpec((None, tm, d // LANES, LANES), lambda b, i: (b, i, 0, 0)),
                  pl.BlockSpec((None, 6, d), lambda b, i: (b, 0, 0)),
                  pl.BlockSpec((1, d), lambda b, i: (0, 0))],
        out_specs=pl.BlockSpec((None, tm, d), lambda b, i: (b, i, 0)),
        out_shape=jax.ShapeDtypeStruct((bsz, s, d), F32),
        compiler_params=_params(("arbitrary", "arbitrary"), 40),
        name="residual_final_norm",
    )(x1, y3, mod, g.reshape(1, d))


def kernel(x, c, positions, w_ada, b_ada, g_norm1, w_in, g_head, hg_lower_bounds, w_attn_up, w_hgrn_up, w_out,
           g_norm2, w_router_group, b_router_group, w_router_expert, b_router_expert, w_exp_gate, w_exp_up,
           w_exp_down, g_final):
    depth = w_ada.shape[0]
    d = x.shape[-1]
    assert depth == 1
    hg_w = HG_HEADS * HG_DIM
    for l in range(depth):
        mod = _modulation(c, w_ada[l], b_ada[l])
        w_l = w_in[l]
        w_att = jnp.pad(w_l[:, :ATT_COLS], ((0, 0), (0, ATT_WIDTH - ATT_COLS))).astype(BF16)
        w_hg = w_l[:, ATT_COLS:ATT_COLS + 4 * hg_w].astype(BF16)
        w_gate = w_l[:, ATT_COLS + 4 * hg_w:].astype(BF16)
        z_att = _norm_matmul(x, mod, g_norm1[l], w_att, ATT_WIDTH)
        z_hg = _norm_matmul(x, mod, g_norm1[l], w_hg, 1024)
        z_gate = _norm_matmul(x, mod, g_norm1[l], w_gate, 1024)
        y_att = _attention(z_att, positions)
        y_hg = _hgrn(z_hg, hg_lower_bounds, g_head[l], l)
        w_r = jnp.pad(jnp.concatenate([w_router_expert[l], w_router_group[l]], axis=1),
                      ((0, 0), (0, LANES - N_EXPERTS - N_GROUPS)))
        b_r = jnp.pad(jnp.concatenate([b_router_expert[l], b_router_group[l]]),
                      (0, LANES - N_EXPERTS - N_GROUPS)).reshape(1, LANES)
        x1, h3, cw3 = _merge(y_att, y_hg, z_gate, x, mod, g_norm2[l],
                             w_attn_up[l].astype(BF16), w_hgrn_up[l].astype(BF16), w_out[l].astype(BF16), w_r, b_r)
        bsz, s = x.shape[:2]
        y3 = _moe(h3.reshape(bsz * s, d // LANES, LANES), cw3.reshape(bsz * s, SUBLANES, LANES),
                  w_exp_gate[l].astype(BF16), w_exp_up[l].astype(BF16), w_exp_down[l].astype(BF16))
        x = _final(x1, y3.reshape(bsz, s, d // LANES, LANES), mod, g_final)
    return x
```

```python
import functools
import math

import jax
import jax.numpy as jnp
from jax import lax
from jax.experimental import pallas as pl
from jax.experimental.pallas import tpu as pltpu

F32 = jnp.float32
BF16 = jnp.bfloat16
I32 = jnp.int32

ATT_HEADS = 8
ATT_KV_HEADS = 2
ATT_GROUP = ATT_HEADS // ATT_KV_HEADS
HEAD_DIM = 128
IDX_HEADS = 8
IDX_DIM = 64
TOPK_MAX = 256
Q_BLOCK = 128
KEY_CHUNK = 512
ROPE_THETA = 10000.0
ATT_SCALE = HEAD_DIM ** -0.5
IDX_SCALE = (IDX_HEADS * IDX_DIM) ** -0.5
HG_HEADS = 8
HG_DIM = 128
HG_CHUNK = 64
N_GROUPS = 4
EXPERTS_PER_GROUP = 8
N_EXPERTS = N_GROUPS * EXPERTS_PER_GROUP
EPS = 1e-6

LANES = 128
SUBLANES = 8
INT_MIN = -(2 ** 31)
NEG_INF_KEY = INT_MIN + 0x7FFFFF
MIB = 1024 * 1024

AQ_OFF = 0
AK_OFF = ATT_HEADS * HEAD_DIM
AV_OFF = AK_OFF + ATT_KV_HEADS * HEAD_DIM
IQ_OFF = AV_OFF + ATT_KV_HEADS * HEAD_DIM
IK_OFF = IQ_OFF + IDX_HEADS * IDX_DIM
ATT_COLS = IK_OFF + IDX_DIM + IDX_HEADS
ATT_WIDTH = IK_OFF + LANES


def _sigmoid(x):
    return 1.0 / (1.0 + jnp.exp(-x))


def _silu(x):
    return x * _sigmoid(x)


def _params(sem, vmem_mib):
    return pltpu.CompilerParams(dimension_semantics=sem, vmem_limit_bytes=vmem_mib * MIB)


def _mod_kernel(c_ref, w_ref, b_ref, o_ref):
    c = c_ref[...]
    ca = _silu(c).astype(BF16)
    o_ref[...] = jnp.dot(ca, w_ref[...].astype(BF16), preferred_element_type=F32) + b_ref[...]


def _modulation(c, w, b):
    bsz, d = c.shape
    n = w.shape[1]
    rows = 8
    cp = jnp.zeros((rows, d), F32).at[:bsz].set(c)
    tn = 1024
    out = pl.pallas_call(
        _mod_kernel,
        grid=(n // tn,),
        in_specs=[pl.BlockSpec((rows, d), lambda j: (0, 0)),
                  pl.BlockSpec((d, tn), lambda j: (0, j)),
                  pl.BlockSpec((1, tn), lambda j: (0, j))],
        out_specs=pl.BlockSpec((rows, tn), lambda j: (0, j)),
        out_shape=jax.ShapeDtypeStruct((rows, n), F32),
        compiler_params=_params(("arbitrary",), 40),
        name="modulation",
    )(cp, w, b.reshape(1, n))
    return out[:bsz].reshape(bsz, 6, d)


def _ada_norm(x, g, shift, scale):
    ms = jnp.mean(x * x, axis=-1, keepdims=True)
    y = x * lax.rsqrt(ms + EPS) * g
    return y * (1.0 + scale) + shift


def _norm_matmul_kernel(x_ref, mod_ref, g_ref, w_ref, o_ref, h_ref):
    @pl.when(pl.program_id(2) == 0)
    def _():
        h = _ada_norm(x_ref[...], g_ref[...], mod_ref[0:1, :], mod_ref[1:2, :])
        h_ref[...] = h.astype(BF16)

    o_ref[...] = jnp.dot(h_ref[...], w_ref[...], preferred_element_type=F32)


def _norm_matmul(x, mod, g, w, tn):
    bsz, s, d = x.shape
    n = w.shape[1]
    tm = min(1024 if tn <= 1024 else 512, s)
    return pl.pallas_call(
        _norm_matmul_kernel,
        grid=(bsz, s // tm, n // tn),
        in_specs=[pl.BlockSpec((None, tm, d), lambda b, i, j: (b, i, 0)),
                  pl.BlockSpec((None, 6, d), lambda b, i, j: (b, 0, 0)),
                  pl.BlockSpec((1, d), lambda b, i, j: (0, 0)),
                  pl.BlockSpec((d, tn), lambda b, i, j: (0, j))],
        out_specs=pl.BlockSpec((None, tm, tn), lambda b, i, j: (b, i, j)),
        out_shape=jax.ShapeDtypeStruct((bsz, s, n), F32),
        scratch_shapes=[pltpu.VMEM((tm, d), BF16)],
        compiler_params=_params(("arbitrary", "arbitrary", "arbitrary"), 48),
        name="norm_matmul",
    )(x, mod, g.reshape(1, d), w)


def _sortable(score):
    bits = pltpu.bitcast(score + 0.0, I32)
    return jnp.where(bits < 0, bits ^ 0x7FFFFFFF, bits)


def _attn_kernel(pos_ref, z_ref, o_ref, k_s, vt_s, ik_s, key_s, bias_s, lg_s, acc_s, thr_s, *, topk):
    i = pl.program_id(1)
    qb = Q_BLOCK
    kc = KEY_CHUNK
    n_chunks = (i * qb + kc) // kc
    r0 = pl.multiple_of(i * qb, qb)

    @pl.when(jnp.logical_and(pl.program_id(0) == 0, i == 0))
    def _():
        k_s[...] = jnp.zeros_like(k_s)
        vt_s[...] = jnp.zeros_like(vt_s)
        ik_s[...] = jnp.zeros_like(ik_s)

    lane = lax.broadcasted_iota(I32, (1, LANES), 1)
    pos = pos_ref[...].astype(F32)
    inv_a = jnp.exp((lane % (HEAD_DIM // 2)).astype(F32) * (-2.0 * math.log(ROPE_THETA) / HEAD_DIM))
    ang_a = pos * inv_a
    cos_a = jnp.cos(ang_a)
    sin_a = jnp.where(lane < HEAD_DIM // 2, -1.0, 1.0) * jnp.sin(ang_a)
    inv_i = jnp.exp((lane % (IDX_DIM // 2)).astype(F32) * (-2.0 * math.log(ROPE_THETA) / IDX_DIM))
    ang_i = pos * inv_i
    low_i = (lane % IDX_DIM) < IDX_DIM // 2
    cos_i = jnp.cos(ang_i)
    sin_i = jnp.where(low_i, -1.0, 1.0) * jnp.sin(ang_i)

    def rope_a(x):
        return x * cos_a + pltpu.roll(x, HEAD_DIM // 2, 1) * sin_a

    def rope_i(x):
        partner = jnp.where(low_i, pltpu.roll(x, LANES - IDX_DIM // 2, 1), pltpu.roll(x, IDX_DIM // 2, 1))
        return x * cos_i + partner * sin_i

    for h in range(ATT_KV_HEADS):
        kh = rope_a(z_ref[:, AK_OFF + h * HEAD_DIM:AK_OFF + (h + 1) * HEAD_DIM])
        k_s[pl.ds(r0, qb), h * HEAD_DIM:(h + 1) * HEAD_DIM] = kh.astype(BF16)
    v = z_ref[:, AV_OFF:AV_OFF + ATT_KV_HEADS * HEAD_DIM]
    vt_s[i] = v.T.astype(BF16)
    tile = z_ref[:, IK_OFF:IK_OFF + LANES]
    ikr = rope_i(tile)
    ik2 = jnp.where(lane < IDX_DIM, ikr, pltpu.roll(ikr, IDX_DIM, 1))
    ik_s[pl.ds(r0, qb), :] = ik2.astype(BF16)
    w_t = tile.T[IDX_DIM:IDX_DIM + IDX_HEADS, :] * IDX_SCALE

    q_rows = []
    for t in range(IDX_HEADS // 2):
        iqr = rope_i(z_ref[:, IQ_OFF + t * LANES:IQ_OFF + (t + 1) * LANES])
        q_rows.append(jnp.where(lane < IDX_DIM, iqr, 0.0).astype(BF16))
        q_rows.append(jnp.where(lane < IDX_DIM, 0.0, iqr).astype(BF16))
    iq_all = jnp.concatenate(q_rows, axis=0)

    key_iota = lax.broadcasted_iota(I32, (kc, LANES), 0)
    q_idx = i * qb + lane

    def chunk_rows(j):
        return pl.ds(pl.multiple_of(j * kc, kc), kc)

    zero8 = jnp.zeros((SUBLANES, LANES), I32)

    def count8(mask):
        return jnp.sum(jnp.where(mask, 1, 0).reshape(kc // SUBLANES, SUBLANES, LANES), axis=0)

    def score_body(j, carry):
        dots = lax.dot_general(ik_s[chunk_rows(j), :], iq_all, (((1,), (1,)), ((), ())),
                               preferred_element_type=F32)
        sc = jnp.zeros((kc, LANES), F32)
        for h in range(IDX_HEADS):
            sc = sc + jnp.maximum(dots[:, h * qb:(h + 1) * qb], 0.0) * w_t[h:h + 1, :]
        sc = jnp.where(key_iota + j * kc <= q_idx, sc, -jnp.inf)
        key_s[chunk_rows(j), :] = _sortable(sc)
        return carry

    lax.fori_loop(0, n_chunks, score_body, 0)

    thr_s[...] = jnp.full((1, LANES), NEG_INF_KEY + 1, I32)
    select = i >= topk // qb

    @pl.when(select)
    def _():
        def bit_body(b, t_u):
            cand = t_u | jnp.left_shift(jnp.int32(1), 31 - b)
            cand_s = cand ^ INT_MIN

            def cnt_body(j, acc):
                return acc + count8(key_s[chunk_rows(j), :] >= cand_s)

            cnt = jnp.sum(lax.fori_loop(0, n_chunks, cnt_body, zero8), axis=0, keepdims=True)
            return jnp.where(cnt >= topk, cand, t_u)

        t_u = lax.fori_loop(0, 32, bit_body, jnp.zeros((1, LANES), I32))
        thr_s[...] = t_u ^ INT_MIN

    thr = thr_s[...]

    def bias_body(j, cnt):
        sel = key_s[chunk_rows(j), :] >= thr
        bias_s[chunk_rows(j), :] = jnp.where(sel, 0.0, -jnp.inf)
        return cnt + count8(sel)

    cnt_ge = jnp.sum(lax.fori_loop(0, n_chunks, bias_body, zero8), axis=0, keepdims=True)
    has_ties = jnp.logical_and(select, jnp.max(cnt_ge) > topk)

    @pl.when(has_ties)
    def _():
        def eq_body(j, acc):
            return acc + count8(key_s[chunk_rows(j), :] == thr)

        cnt_eq = jnp.sum(lax.fori_loop(0, n_chunks, eq_body, zero8), axis=0, keepdims=True)
        n_keep = (topk - (cnt_ge - cnt_eq)).astype(F32)
        tril = (lax.broadcasted_iota(I32, (kc, kc), 1) <= lax.broadcasted_iota(I32, (kc, kc), 0)).astype(BF16)

        def tie_body(j, run):
            kk = key_s[chunk_rows(j), :]
            eq = kk == thr
            pref = jnp.dot(tril, jnp.where(eq, 1.0, 0.0).astype(BF16), preferred_element_type=F32) + run
            sel = jnp.logical_or(kk > thr, jnp.logical_and(eq, pref <= n_keep))
            bias_s[chunk_rows(j), :] = jnp.where(sel, 0.0, -jnp.inf)
            return pref[kc - 1:kc, :]

        lax.fori_loop(0, n_chunks, tie_body, jnp.zeros((1, LANES), F32))

    gw = ATT_GROUP * qb
    for h in range(ATT_KV_HEADS):
        qs = jnp.concatenate(
            [(rope_a(z_ref[:, AQ_OFF + (h * ATT_GROUP + g) * HEAD_DIM:AQ_OFF + (h * ATT_GROUP + g + 1) * HEAD_DIM])
              * ATT_SCALE).astype(BF16) for g in range(ATT_GROUP)], axis=0)

        def logit_body(j, m8):
            kch = k_s[chunk_rows(j), h * HEAD_DIM:(h + 1) * HEAD_DIM]
            lt = lax.dot_general(kch, qs, (((1,), (1,)), ((), ())), preferred_element_type=F32)
            bias = bias_s[chunk_rows(j), :]
            lm = lt + jnp.concatenate([bias] * ATT_GROUP, axis=1)
            lg_s[chunk_rows(j), :] = lm
            return jnp.maximum(m8, jnp.max(lm.reshape(kc // 8, 8, gw), axis=0))

        m8 = lax.fori_loop(0, n_chunks, logit_body, jnp.full((8, gw), -jnp.inf, F32))
        m = jnp.max(m8, axis=0, keepdims=True)
        acc_s[...] = jnp.zeros_like(acc_s)

        def pv_body(j, l8):
            p = jnp.exp(lg_s[chunk_rows(j), :] - m)
            vt = jnp.concatenate([vt_s[(kc // qb) * j + t][h * HEAD_DIM:(h + 1) * HEAD_DIM, :]
                                  for t in range(kc // qb)], axis=1)
            acc_s[...] += jnp.dot(vt, p.astype(BF16), preferred_element_type=F32)
            return l8 + jnp.sum(p.reshape(kc // 8, 8, gw), axis=0)

        l8 = lax.fori_loop(0, n_chunks, pv_body, jnp.zeros((8, gw), F32))
        o_t = acc_s[...] / jnp.sum(l8, axis=0, keepdims=True)
        for g in range(ATT_GROUP):
            c0 = (h * ATT_GROUP + g) * HEAD_DIM
            o_ref[:, c0:c0 + HEAD_DIM] = o_t[:, g * qb:(g + 1) * qb].T.astype(o_ref.dtype)


def _attention(z_att, positions):
    bsz, s, _ = z_att.shape
    nblk = s // Q_BLOCK
    topk = min(TOPK_MAX, s // 4)
    assert s % KEY_CHUNK == 0 and KEY_CHUNK % Q_BLOCK == 0 and topk % Q_BLOCK == 0
    return pl.pallas_call(
        functools.partial(_attn_kernel, topk=topk),
        grid=(bsz, nblk),
        in_specs=[pl.BlockSpec((None, Q_BLOCK, 1), lambda b, i: (b, i, 0)),
                  pl.BlockSpec((None, Q_BLOCK, ATT_WIDTH), lambda b, i: (b, i, 0))],
        out_specs=pl.BlockSpec((None, Q_BLOCK, ATT_HEADS * HEAD_DIM), lambda b, i: (b, i, 0)),
        out_shape=jax.ShapeDtypeStruct((bsz, s, ATT_HEADS * HEAD_DIM), BF16),
        scratch_shapes=[pltpu.VMEM((s, ATT_KV_HEADS * HEAD_DIM), BF16),
                        pltpu.VMEM((nblk, ATT_KV_HEADS * HEAD_DIM, Q_BLOCK), BF16),
                        pltpu.VMEM((s, LANES), BF16),
                        pltpu.VMEM((s, LANES), I32),
                        pltpu.VMEM((s, LANES), F32),
                        pltpu.VMEM((s, ATT_GROUP * Q_BLOCK), F32),
                        pltpu.VMEM((HEAD_DIM, ATT_GROUP * Q_BLOCK), F32),
                        pltpu.VMEM((1, LANES), I32)],
        compiler_params=_params(("arbitrary", "arbitrary"), 48),
        name="dsa_attention",
    )(positions.reshape(bsz, s, 1), z_att)


def _hgrn_kernel(lbp_ref, gh_ref, q_ref, f_ref, i_ref, og_ref, o_ref, *, layer, n_chunks):
    h = pl.program_id(1)
    c = HG_CHUNK
    lbp = lbp_ref[:, pl.ds(h, 1), :]
    e = jnp.exp(lbp - jnp.max(lbp, axis=0, keepdims=True))
    sm = e / jnp.sum(e, axis=0, keepdims=True)
    lb = sm[0]
    for r in range(1, layer + 1):
        lb = lb + sm[r]
    g_head = gh_ref[pl.ds(h, 1), :]

    t_io = lax.broadcasted_iota(I32, (c, c), 0)
    s_io = lax.broadcasted_iota(I32, (c, c), 1)
    tril = s_io <= t_io
    tril_f = tril.astype(F32)

    def chunk_body(n, st):
        rows = pl.ds(pl.multiple_of(n * c, c), c)
        q = _silu(q_ref[rows, :])
        f = lb + (1.0 - lb) * _sigmoid(f_ref[rows, :])
        log_f = jnp.log(f)
        k = 1.0 - f
        v = i_ref[rows, :]
        cum = jnp.dot(tril_f, log_f, preferred_element_type=F32, precision=lax.Precision.HIGHEST)
        cum_last = cum[c - 1:c, :]
        q_dec = (q * jnp.exp(cum)).astype(BF16)
        k_dec = (k * jnp.exp(-cum)).astype(BF16)
        att = lax.dot_general(q_dec, k_dec, (((1,), (1,)), ((), ())), preferred_element_type=F32)
        att = jnp.where(tril, att, 0.0)
        v16 = v.astype(BF16)
        intra = jnp.dot(att.astype(BF16), v16, preferred_element_type=F32)
        inter = lax.dot_general(q_dec, st.astype(BF16), (((1,), (1,)), ((), ())), preferred_element_type=F32)
        o = intra + inter
        y = o * lax.rsqrt(jnp.mean(o * o, axis=-1, keepdims=True) + EPS) * g_head
        o_ref[rows, :] = (y * _silu(og_ref[rows, :])).astype(o_ref.dtype)
        k_end = (k * jnp.exp(cum_last - cum)).astype(BF16)
        upd_t = jnp.dot(v.T.astype(BF16), k_end, preferred_element_type=F32)
        return jnp.exp(cum_last) * st + upd_t

    lax.fori_loop(0, n_chunks, chunk_body, jnp.zeros((HG_DIM, HG_DIM), F32), unroll=8)


def _hgrn(z_hg, lower_bounds, g_head, layer):
    bsz, s, _ = z_hg.shape
    depth1 = lower_bounds.shape[0]
    col = lambda sec: (lambda b, h: (b, 0, sec * HG_HEADS + h))
    specs = [pl.BlockSpec((None, s, HG_DIM), col(sec)) for sec in range(4)]
    return pl.pallas_call(
        functools.partial(_hgrn_kernel, layer=layer, n_chunks=s // HG_CHUNK),
        grid=(bsz, HG_HEADS),
        in_specs=[pl.BlockSpec((depth1, HG_HEADS, HG_DIM), lambda b, h: (0, 0, 0)),
                  pl.BlockSpec((HG_HEADS, HG_DIM), lambda b, h: (0, 0))] + specs,
        out_specs=pl.BlockSpec((None, s, HG_DIM), lambda b, h: (b, 0, h)),
        out_shape=jax.ShapeDtypeStruct((bsz, s, HG_HEADS * HG_DIM), BF16),
        compiler_params=_params(("arbitrary", "arbitrary"), 48),
        name="hgrn2",
    )(lower_bounds.reshape(depth1, HG_HEADS, HG_DIM), g_head, z_hg, z_hg, z_hg, z_hg)


def _merge_kernel(ya_ref, yh_ref, ga_ref, gh_ref, x_ref, mod_ref, g2_ref, wa_ref, wh_ref, wo_ref, wr_ref, br_ref,
                  x1_ref, h3_ref, cw3_ref):
    a = jnp.dot(ya_ref[...], wa_ref[...], preferred_element_type=F32)
    hg = jnp.dot(yh_ref[...], wh_ref[...], preferred_element_type=F32)
    merged = _sigmoid(ga_ref[...]) * a + _sigmoid(gh_ref[...]) * hg
    x1 = x_ref[...] + mod_ref[2:3, :] * jnp.dot(merged.astype(BF16), wo_ref[...], preferred_element_type=F32)
    x1_ref[...] = x1
    h2 = _ada_norm(x1, g2_ref[...], mod_ref[3:4, :], mod_ref[4:5, :])
    for t in range(h2.shape[1] // LANES):
        h3_ref[:, t, :] = h2[:, t * LANES:(t + 1) * LANES]

    lg = jnp.dot(h2, wr_ref[...], preferred_element_type=F32, precision=lax.Precision.HIGHEST) + br_ref[...]
    lane = lax.broadcasted_iota(I32, lg.shape, 1)
    neg = -jnp.inf
    is_grp = jnp.logical_and(lane >= N_EXPERTS, lane < N_EXPERTS + N_GROUPS)
    g_lg = jnp.where(is_grp, lg, neg)
    g_max = jnp.max(g_lg, axis=-1, keepdims=True)
    p_grp = 1.0 / jnp.sum(jnp.exp(g_lg - g_max), axis=-1, keepdims=True)
    grp = jnp.min(jnp.where(g_lg == g_max, lane, 2 * LANES), axis=-1, keepdims=True) - N_EXPERTS
    lo = grp * EXPERTS_PER_GROUP
    in_grp = jnp.logical_and(lane >= lo, lane < lo + EXPERTS_PER_GROUP)
    e_lg = jnp.where(in_grp, lg, neg)
    e_max = jnp.max(e_lg, axis=-1, keepdims=True)
    idx1 = jnp.min(jnp.where(e_lg == e_max, lane, 2 * LANES), axis=-1, keepdims=True)
    e_lg2 = jnp.where(lane == idx1, neg, e_lg)
    e_max2 = jnp.max(e_lg2, axis=-1, keepdims=True)
    idx2 = jnp.min(jnp.where(e_lg2 == e_max2, lane, 2 * LANES), axis=-1, keepdims=True)
    e2 = jnp.exp(e_max2 - e_max)
    p1 = p_grp / (1.0 + e2)
    p2 = p_grp * e2 / (1.0 + e2)
    cw = jnp.where(lane == idx1, p1, jnp.where(lane == idx2, p2, jnp.where(lane == grp + N_EXPERTS, 1.0, 0.0)))
    cw3_ref[...] = jnp.zeros_like(cw3_ref)
    cw3_ref[:, 0, :] = cw


def _merge(y_att, y_hg, z_gate, x, mod, g2, wa, wh, wo, wr, br):
    bsz, s, d = x.shape
    tm = min(256, s)
    dh = y_att.shape[-1]
    const = lambda shape: pl.BlockSpec(shape, lambda b, i: (0,) * len(shape))
    row = lambda width, j=0: pl.BlockSpec((None, tm, width), lambda b, i: (b, i, j))
    slab = lambda rows: pl.BlockSpec((None, tm, rows, LANES), lambda b, i: (b, i, 0, 0))
    return pl.pallas_call(
        _merge_kernel,
        grid=(bsz, s // tm),
        in_specs=[row(dh), row(dh), row(d, 0), row(d, 1), row(d),
                  pl.BlockSpec((None, 6, d), lambda b, i: (b, 0, 0)), const((1, d)),
                  const((dh, d)), const((dh, d)), const((d, d)), const((d, LANES)), const((1, LANES))],
        out_specs=[row(d), slab(d // LANES), slab(SUBLANES)],
        out_shape=[jax.ShapeDtypeStruct((bsz, s, d), F32),
                   jax.ShapeDtypeStruct((bsz, s, d // LANES, LANES), F32),
                   jax.ShapeDtypeStruct((bsz, s, SUBLANES, LANES), F32)],
        compiler_params=_params(("arbitrary", "arbitrary"), 56),
        name="merge_router",
    )(y_att, y_hg, z_gate, z_gate, x, mod, g2.reshape(1, d), wa, wh, wo, wr, br)


def _moe_kernel(tg_ref, nv_ref, src_ref, h3_hbm, cw3_hbm, wg_ref, wu_ref, wd_ref, y3_hbm,
                hbuf, cbuf, xb, cwv, acc, ybuf, gsem, ssem, *, tm):
    k = pl.program_id(0)
    e = pl.program_id(1)
    nt = pl.num_programs(0)
    ne = pl.num_programs(1)
    slot = k % 2
    nv = nv_ref[k]
    n_slab = xb.shape[1] // LANES

    def gather_copies(tile, slot_, r):
        tok = src_ref[tile * tm + r]
        return (pltpu.make_async_copy(h3_hbm.at[tok], hbuf.at[slot_, r], gsem.at[slot_]),
                pltpu.make_async_copy(cw3_hbm.at[tok], cbuf.at[slot_, r], gsem.at[slot_]))

    def gather(tile, slot_, start):
        def body(r, carry):
            for cp in gather_copies(tile, slot_, r):
                cp.start() if start else cp.wait()
            return carry

        lax.fori_loop(0, tm, body, 0, unroll=8)

    def scatter(tile, start):
        def body(r, carry):
            cp = pltpu.make_async_copy(ybuf.at[r], y3_hbm.at[src_ref[tile * tm + r]], ssem.at[0])
            cp.start() if start else cp.wait()
            return carry

        lax.fori_loop(0, nv_ref[tile], body, 0)

    @pl.when(e == 0)
    def _():
        @pl.when(jnp.logical_and(k == 0, nv > 0))
        def _():
            gather(k, slot, True)

        nxt = jnp.minimum(k + 1, nt - 1)

        @pl.when(jnp.logical_and(k + 1 < nt, nv_ref[nxt] > 0))
        def _():
            gather(nxt, 1 - slot, True)

        @pl.when(nv > 0)
        def _():
            gather(k, slot, False)
            for t in range(n_slab):
                xb[:, t * LANES:(t + 1) * LANES] = hbuf[slot, :, t, :].astype(BF16)
            cwv[...] = cbuf[slot, :, 0, :]
            acc[...] = jnp.zeros_like(acc)

    @pl.when(nv > 0)
    def _():
        h = xb[...]
        a = jnp.dot(h, wg_ref[...].astype(BF16), preferred_element_type=F32)
        u = jnp.dot(h, wu_ref[...].astype(BF16), preferred_element_type=F32)
        cw = cwv[...]
        lane = lax.broadcasted_iota(I32, cw.shape, 1)
        cwe = jnp.sum(jnp.where(lane == tg_ref[k] * ne + e, cw, 0.0), axis=-1, keepdims=True)
        hm = (_silu(a) * u * cwe).astype(BF16)
        acc[...] += jnp.dot(hm, wd_ref[...].astype(BF16), preferred_element_type=F32)

    @pl.when(e == ne - 1)
    def _():
        @pl.when(k > 0)
        def _():
            scatter(jnp.maximum(k - 1, 0), False)

        @pl.when(nv > 0)
        def _():
            for t in range(n_slab):
                ybuf[:, t, :] = acc[:, t * LANES:(t + 1) * LANES]
            scatter(k, True)

        @pl.when(k == nt - 1)
        def _():
            scatter(k, False)


def _moe_tiles(grp, tm):
    t = grp.shape[0]
    nt = t // tm + N_GROUPS
    order = jnp.argsort(grp, stable=True).astype(I32)
    counts = jnp.sum((grp[:, None] == jnp.arange(N_GROUPS, dtype=I32)[None, :]).astype(I32), axis=0)
    tiles_g = (counts + tm - 1) // tm
    tile_end = jnp.cumsum(tiles_g)
    tile0 = tile_end - tiles_g
    gstart = jnp.cumsum(counts) - counts
    k = jnp.arange(nt, dtype=I32)
    used = k < tile_end[-1]
    tg = jnp.sum((k[:, None] >= tile_end[None, :]).astype(I32), axis=1)
    last_g = jnp.sum((tile_end[-1] - 1 >= tile_end).astype(I32))
    tg = jnp.where(used, tg, last_g)
    q0 = (k - tile0[tg]) * tm
    nvalid = jnp.where(used, jnp.clip(counts[tg] - q0, 0, tm), 0).astype(I32)
    r = jnp.arange(tm, dtype=I32)
    idx = jnp.clip(gstart[tg][:, None] + q0[:, None] + r[None, :], 0, t - 1)
    src = jnp.where(r[None, :] < nvalid[:, None], order[idx], 0).astype(I32)
    return tg.astype(I32), nvalid, src.reshape(nt * tm)


def _moe(h3, cw3, wg, wu, wd):
    t, n_slab, _ = h3.shape
    d = n_slab * LANES
    ne_all, _, ff = wg.shape
    tm = min(512, t)
    grp = jnp.argmax(cw3[:, 0, N_EXPERTS:N_EXPERTS + N_GROUPS], axis=-1).astype(I32)
    tg, nvalid, src = _moe_tiles(grp, tm)
    nt = tg.shape[0]
    w_idx = lambda k, e, tg_r, nv_r, src_r: (tg_r[k] * EXPERTS_PER_GROUP + jnp.where(nv_r[k] > 0, e, EXPERTS_PER_GROUP - 1),
                                           0, 0)
    grid_spec = pltpu.PrefetchScalarGridSpec(
        num_scalar_prefetch=3,
        grid=(nt, EXPERTS_PER_GROUP),
        in_specs=[pl.BlockSpec(memory_space=pl.ANY), pl.BlockSpec(memory_space=pl.ANY),
                  pl.BlockSpec((None, d, ff), w_idx), pl.BlockSpec((None, d, ff), w_idx),
                  pl.BlockSpec((None, ff, d), w_idx)],
        out_specs=pl.BlockSpec(memory_space=pl.ANY),
        scratch_shapes=[pltpu.VMEM((2, tm, n_slab, LANES), F32),
                        pltpu.VMEM((2, tm, SUBLANES, LANES), F32),
                        pltpu.VMEM((tm, d), BF16),
                        pltpu.VMEM((tm, LANES), F32),
                        pltpu.VMEM((tm, d), F32),
                        pltpu.VMEM((tm, n_slab, LANES), F32),
                        pltpu.SemaphoreType.DMA((2,)),
                        pltpu.SemaphoreType.DMA((1,))])
    return pl.pallas_call(
        functools.partial(_moe_kernel, tm=tm),
        grid_spec=grid_spec,
        out_shape=jax.ShapeDtypeStruct((t, n_slab, LANES), F32),
        compiler_params=_params(("arbitrary", "arbitrary"), 56),
        name="moe_experts",
    )(tg, nvalid, src, h3, cw3, wg, wu, wd)


def _final_kernel(x1_ref, y3_ref, mod_ref, g_ref, o_ref, *, gate_row):
    y = jnp.concatenate([y3_ref[:, t, :] for t in range(y3_ref.shape[1])], axis=1)
    x2 = x1_ref[...] + mod_ref[gate_row:gate_row + 1, :] * y
    ms = jnp.mean(x2 * x2, axis=-1, keepdims=True)
    o_ref[...] = x2 * lax.rsqrt(ms + EPS) * g_ref[...]


def _final(x1, y3, mod, g):
    bsz, s, d = x1.shape
    tm = min(512, s)
    return pl.pallas_call(
        functools.partial(_final_kernel, gate_row=5),
        grid=(bsz, s // tm),
        in_specs=[pl.BlockSpec((None, tm, d), lambda b, i: (b, i, 0)),
                  pl.BlockSpec((None, tm, d // LANES, LANES), lambda b, i: (b, i, 0, 0)),
                  pl.BlockSpec((None, 6, d), lambda b, i: (b, 0, 0)),
                  pl.BlockSpec((1, d), lambda b, i: (0, 0))],
        out_specs=pl.BlockSpec((None, tm, d), lambda b, i: (b, i, 0)),
        out_shape=jax.ShapeDtypeStruct((bsz, s, d), F32),
        compiler_params=_params(("arbitrary", "arbitrary"), 40),
        name="residual_final_norm",
    )(x1, y3, mod, g.reshape(1, d))


def kernel(x, c, positions, w_ada, b_ada, g_norm1, w_in, g_head, hg_lower_bounds, w_attn_up, w_hgrn_up, w_out,
           g_norm2, w_router_group, b_router_group, w_router_expert, b_router_expert, w_exp_gate, w_exp_up,
           w_exp_down, g_final):
    depth = w_ada.shape[0]
    d = x.shape[-1]
    assert depth == 1
    hg_w = HG_HEADS * HG_DIM
    for l in range(depth):
        mod = _modulation(c, w_ada[l], b_ada[l])
        w_l = w_in[l]
        w_att = jnp.pad(w_l[:, :ATT_COLS], ((0, 0), (0, ATT_WIDTH - ATT_COLS))).astype(BF16)
        w_hg = w_l[:, ATT_COLS:ATT_COLS + 4 * hg_w].astype(BF16)
        w_gate = w_l[:, ATT_COLS + 4 * hg_w:].astype(BF16)
        z_att = _norm_matmul(x, mod, g_norm1[l], w_att, ATT_WIDTH)
        z_hg = _norm_matmul(x, mod, g_norm1[l], w_hg, 1024)
        z_gate = _norm_matmul(x, mod, g_norm1[l], w_gate, 1024)
        y_att = _attention(z_att, positions)
        y_hg = _hgrn(z_hg, hg_lower_bounds, g_head[l], l)
        w_r = jnp.pad(jnp.concatenate([w_router_expert[l], w_router_group[l]], axis=1),
                      ((0, 0), (0, LANES - N_EXPERTS - N_GROUPS)))
        b_r = jnp.pad(jnp.concatenate([b_router_expert[l], b_router_group[l]]),
                      (0, LANES - N_EXPERTS - N_GROUPS)).reshape(1, LANES)
        x1, h3, cw3 = _merge(y_att, y_hg, z_gate, x, mod, g_norm2[l],
                             w_attn_up[l].astype(BF16), w_hgrn_up[l].astype(BF16), w_out[l].astype(BF16), w_r, b_r)
        bsz, s = x.shape[:2]
        y3 = _moe(h3.reshape(bsz * s, d // LANES, LANES), cw3.reshape(bsz * s, SUBLANES, LANES),
                  w_exp_gate[l], w_exp_up[l], w_exp_down[l])
        x = _final(x1, y3.reshape(bsz, s, d // LANES, LANES), mod, g_final)
    return x
```

```python
import functools
import math

import jax
import jax.numpy as jnp
from jax import lax
from jax.experimental import pallas as pl
from jax.experimental.pallas import tpu as pltpu

F32 = jnp.float32
BF16 = jnp.bfloat16
I32 = jnp.int32

ATT_HEADS = 8
ATT_KV_HEADS = 2
ATT_GROUP = ATT_HEADS // ATT_KV_HEADS
HEAD_DIM = 128
IDX_HEADS = 8
IDX_DIM = 64
TOPK_MAX = 256
Q_BLOCK = 128
KEY_CHUNK = 512
ROPE_THETA = 10000.0
ATT_SCALE = HEAD_DIM ** -0.5
IDX_SCALE = (IDX_HEADS * IDX_DIM) ** -0.5
HG_HEADS = 8
HG_DIM = 128
HG_CHUNK = 64
HG_GROUP_ROWS = 512
N_GROUPS = 4
EXPERTS_PER_GROUP = 8
N_EXPERTS = N_GROUPS * EXPERTS_PER_GROUP
EPS = 1e-6

LANES = 128
SUBLANES = 8
INT_MIN = -(2 ** 31)
NEG_INF_KEY = INT_MIN + 0x7FFFFF
MIB = 1024 * 1024

AQ_OFF = 0
AK_OFF = ATT_HEADS * HEAD_DIM
AV_OFF = AK_OFF + ATT_KV_HEADS * HEAD_DIM
IQ_OFF = AV_OFF + ATT_KV_HEADS * HEAD_DIM
IK_OFF = IQ_OFF + IDX_HEADS * IDX_DIM
ATT_COLS = IK_OFF + IDX_DIM + IDX_HEADS
ATT_WIDTH = IK_OFF + LANES


def _sigmoid(x):
    return 1.0 / (1.0 + jnp.exp(-x))


def _silu(x):
    return x * _sigmoid(x)


def _params(sem, vmem_mib):
    return pltpu.CompilerParams(dimension_semantics=sem, vmem_limit_bytes=vmem_mib * MIB)


def _mod_kernel(c_ref, w_ref, b_ref, o_ref):
    c = c_ref[...]
    ca = _silu(c).astype(BF16)
    o_ref[...] = jnp.dot(ca, w_ref[...].astype(BF16), preferred_element_type=F32) + b_ref[...]


def _modulation(c, w, b):
    bsz, d = c.shape
    n = w.shape[1]
    rows = 8
    cp = jnp.zeros((rows, d), F32).at[:bsz].set(c)
    tn = 1024
    out = pl.pallas_call(
        _mod_kernel,
        grid=(n // tn,),
        in_specs=[pl.BlockSpec((rows, d), lambda j: (0, 0)),
                  pl.BlockSpec((d, tn), lambda j: (0, j)),
                  pl.BlockSpec((1, tn), lambda j: (0, j))],
        out_specs=pl.BlockSpec((rows, tn), lambda j: (0, j)),
        out_shape=jax.ShapeDtypeStruct((rows, n), F32),
        compiler_params=_params(("arbitrary",), 40),
        name="modulation",
    )(cp, w, b.reshape(1, n))
    return out[:bsz].reshape(bsz, 6, d)


def _ada_norm(x, g, shift, scale):
    ms = jnp.mean(x * x, axis=-1, keepdims=True)
    y = x * lax.rsqrt(ms + EPS) * g
    return y * (1.0 + scale) + shift


def _norm_matmul_kernel(x_ref, mod_ref, g_ref, w_ref, o_ref, h_ref):
    @pl.when(pl.program_id(2) == 0)
    def _():
        h = _ada_norm(x_ref[...], g_ref[...], mod_ref[0:1, :], mod_ref[1:2, :])
        h_ref[...] = h.astype(BF16)

    o_ref[...] = jnp.dot(h_ref[...], w_ref[...], preferred_element_type=F32)


def _norm_matmul(x, mod, g, w, tn):
    bsz, s, d = x.shape
    n = w.shape[1]
    tm = min(1024 if tn <= 1024 else 512, s)
    return pl.pallas_call(
        _norm_matmul_kernel,
        grid=(bsz, s // tm, n // tn),
        in_specs=[pl.BlockSpec((None, tm, d), lambda b, i, j: (b, i, 0)),
                  pl.BlockSpec((None, 6, d), lambda b, i, j: (b, 0, 0)),
                  pl.BlockSpec((1, d), lambda b, i, j: (0, 0)),
                  pl.BlockSpec((d, tn), lambda b, i, j: (0, j))],
        out_specs=pl.BlockSpec((None, tm, tn), lambda b, i, j: (b, i, j)),
        out_shape=jax.ShapeDtypeStruct((bsz, s, n), F32),
        scratch_shapes=[pltpu.VMEM((tm, d), BF16)],
        compiler_params=_params(("arbitrary", "arbitrary", "arbitrary"), 48),
        name="norm_matmul",
    )(x, mod, g.reshape(1, d), w)


def _sortable(score):
    bits = pltpu.bitcast(score + 0.0, I32)
    return jnp.where(bits < 0, bits ^ 0x7FFFFFFF, bits)


def _attn_kernel(pos_ref, z_ref, o_ref, k_s, vt_s, ik_s, key_s, bias_s, lg_s, acc_s, thr_s, *, topk):
    i = pl.program_id(1)
    qb = Q_BLOCK
    kc = KEY_CHUNK
    n_chunks = (i * qb + kc) // kc
    r0 = pl.multiple_of(i * qb, qb)

    @pl.when(jnp.logical_and(pl.program_id(0) == 0, i == 0))
    def _():
        k_s[...] = jnp.zeros_like(k_s)
        vt_s[...] = jnp.zeros_like(vt_s)
        ik_s[...] = jnp.zeros_like(ik_s)

    lane = lax.broadcasted_iota(I32, (1, LANES), 1)
    pos = pos_ref[...].astype(F32)
    inv_a = jnp.exp((lane % (HEAD_DIM // 2)).astype(F32) * (-2.0 * math.log(ROPE_THETA) / HEAD_DIM))
    ang_a = pos * inv_a
    cos_a = jnp.cos(ang_a)
    sin_a = jnp.where(lane < HEAD_DIM // 2, -1.0, 1.0) * jnp.sin(ang_a)
    inv_i = jnp.exp((lane % (IDX_DIM // 2)).astype(F32) * (-2.0 * math.log(ROPE_THETA) / IDX_DIM))
    ang_i = pos * inv_i
    low_i = (lane % IDX_DIM) < IDX_DIM // 2
    cos_i = jnp.cos(ang_i)
    sin_i = jnp.where(low_i, -1.0, 1.0) * jnp.sin(ang_i)

    def rope_a(x):
        return x * cos_a + pltpu.roll(x, HEAD_DIM // 2, 1) * sin_a

    def rope_i(x):
        partner = jnp.where(low_i, pltpu.roll(x, LANES - IDX_DIM // 2, 1), pltpu.roll(x, IDX_DIM // 2, 1))
        return x * cos_i + partner * sin_i

    for h in range(ATT_KV_HEADS):
        kh = rope_a(z_ref[:, AK_OFF + h * HEAD_DIM:AK_OFF + (h + 1) * HEAD_DIM])
        k_s[pl.ds(r0, qb), h * HEAD_DIM:(h + 1) * HEAD_DIM] = kh.astype(BF16)
    v = z_ref[:, AV_OFF:AV_OFF + ATT_KV_HEADS * HEAD_DIM]
    vt_s[i] = v.T.astype(BF16)
    tile = z_ref[:, IK_OFF:IK_OFF + LANES]
    ikr = rope_i(tile)
    ik2 = jnp.where(lane < IDX_DIM, ikr, pltpu.roll(ikr, IDX_DIM, 1))
    ik_s[pl.ds(r0, qb), :] = ik2.astype(BF16)
    w_t = tile.T[IDX_DIM:IDX_DIM + IDX_HEADS, :] * IDX_SCALE

    q_rows = []
    for t in range(IDX_HEADS // 2):
        iqr = rope_i(z_ref[:, IQ_OFF + t * LANES:IQ_OFF + (t + 1) * LANES])
        q_rows.append(jnp.where(lane < IDX_DIM, iqr, 0.0).astype(BF16))
        q_rows.append(jnp.where(lane < IDX_DIM, 0.0, iqr).astype(BF16))
    iq_all = jnp.concatenate(q_rows, axis=0)

    key_iota = lax.broadcasted_iota(I32, (kc, LANES), 0)
    q_idx = i * qb + lane

    def chunk_rows(j):
        return pl.ds(pl.multiple_of(j * kc, kc), kc)

    zero8 = jnp.zeros((SUBLANES, LANES), I32)

    def count8(mask):
        return jnp.sum(jnp.where(mask, 1, 0).reshape(kc // SUBLANES, SUBLANES, LANES), axis=0)

    def score_body(j, carry):
        dots = lax.dot_general(ik_s[chunk_rows(j), :], iq_all, (((1,), (1,)), ((), ())),
                               preferred_element_type=F32)
        sc = jnp.zeros((kc, LANES), F32)
        for h in range(IDX_HEADS):
            sc = sc + jnp.maximum(dots[:, h * qb:(h + 1) * qb], 0.0) * w_t[h:h + 1, :]
        sc = jnp.where(key_iota + j * kc <= q_idx, sc, -jnp.inf)
        key_s[chunk_rows(j), :] = _sortable(sc)
        return carry

    lax.fori_loop(0, n_chunks, score_body, 0)

    thr_s[...] = jnp.full((1, LANES), NEG_INF_KEY + 1, I32)
    select = i >= topk // qb

    @pl.when(select)
    def _():
        def bit_body(b, t_u):
            cand = t_u | jnp.left_shift(jnp.int32(1), 31 - b)
            cand_s = cand ^ INT_MIN

            def cnt_body(j, acc):
                return acc + count8(key_s[chunk_rows(j), :] >= cand_s)

            cnt = jnp.sum(lax.fori_loop(0, n_chunks, cnt_body, zero8), axis=0, keepdims=True)
            return jnp.where(cnt >= topk, cand, t_u)

        t_u = lax.fori_loop(0, 32, bit_body, jnp.zeros((1, LANES), I32))
        thr_s[...] = t_u ^ INT_MIN

    thr = thr_s[...]

    def bias_body(j, cnt):
        sel = key_s[chunk_rows(j), :] >= thr
        bias_s[chunk_rows(j), :] = jnp.where(sel, 0.0, -jnp.inf)
        return cnt + count8(sel)

    cnt_ge = jnp.sum(lax.fori_loop(0, n_chunks, bias_body, zero8), axis=0, keepdims=True)
    has_ties = jnp.logical_and(select, jnp.max(cnt_ge) > topk)

    @pl.when(has_ties)
    def _():
        def eq_body(j, acc):
            return acc + count8(key_s[chunk_rows(j), :] == thr)

        cnt_eq = jnp.sum(lax.fori_loop(0, n_chunks, eq_body, zero8), axis=0, keepdims=True)
        n_keep = (topk - (cnt_ge - cnt_eq)).astype(F32)
        tril = (lax.broadcasted_iota(I32, (kc, kc), 1) <= lax.broadcasted_iota(I32, (kc, kc), 0)).astype(BF16)

        def tie_body(j, run):
            kk = key_s[chunk_rows(j), :]
            eq = kk == thr
            pref = jnp.dot(tril, jnp.where(eq, 1.0, 0.0).astype(BF16), preferred_element_type=F32) + run
            sel = jnp.logical_or(kk > thr, jnp.logical_and(eq, pref <= n_keep))
            bias_s[chunk_rows(j), :] = jnp.where(sel, 0.0, -jnp.inf)
            return pref[kc - 1:kc, :]

        lax.fori_loop(0, n_chunks, tie_body, jnp.zeros((1, LANES), F32))

    gw = ATT_GROUP * qb
    for h in range(ATT_KV_HEADS):
        qs = jnp.concatenate(
            [(rope_a(z_ref[:, AQ_OFF + (h * ATT_GROUP + g) * HEAD_DIM:AQ_OFF + (h * ATT_GROUP + g + 1) * HEAD_DIM])
              * ATT_SCALE).astype(BF16) for g in range(ATT_GROUP)], axis=0)

        def logit_body(j, m8):
            kch = k_s[chunk_rows(j), h * HEAD_DIM:(h + 1) * HEAD_DIM]
            lt = lax.dot_general(kch, qs, (((1,), (1,)), ((), ())), preferred_element_type=F32)
            bias = bias_s[chunk_rows(j), :]
            lm = lt + jnp.concatenate([bias] * ATT_GROUP, axis=1)
            lg_s[chunk_rows(j), :] = lm
            return jnp.maximum(m8, jnp.max(lm.reshape(kc // 8, 8, gw), axis=0))

        m8 = lax.fori_loop(0, n_chunks, logit_body, jnp.full((8, gw), -jnp.inf, F32))
        m = jnp.max(m8, axis=0, keepdims=True)
        acc_s[...] = jnp.zeros_like(acc_s)

        def pv_body(j, l8):
            p = jnp.exp(lg_s[chunk_rows(j), :] - m)
            vt = jnp.concatenate([vt_s[(kc // qb) * j + t][h * HEAD_DIM:(h + 1) * HEAD_DIM, :]
                                  for t in range(kc // qb)], axis=1)
            acc_s[...] += jnp.dot(vt, p.astype(BF16), preferred_element_type=F32)
            return l8 + jnp.sum(p.reshape(kc // 8, 8, gw), axis=0)

        l8 = lax.fori_loop(0, n_chunks, pv_body, jnp.zeros((8, gw), F32))
        o_t = acc_s[...] / jnp.sum(l8, axis=0, keepdims=True)
        for g in range(ATT_GROUP):
            c0 = (h * ATT_GROUP + g) * HEAD_DIM
            o_ref[:, c0:c0 + HEAD_DIM] = o_t[:, g * qb:(g + 1) * qb].T.astype(o_ref.dtype)


def _attention(z_att, positions):
    bsz, s, _ = z_att.shape
    nblk = s // Q_BLOCK
    topk = min(TOPK_MAX, s // 4)
    assert s % KEY_CHUNK == 0 and KEY_CHUNK % Q_BLOCK == 0 and topk % Q_BLOCK == 0
    return pl.pallas_call(
        functools.partial(_attn_kernel, topk=topk),
        grid=(bsz, nblk),
        in_specs=[pl.BlockSpec((None, Q_BLOCK, 1), lambda b, i: (b, i, 0)),
                  pl.BlockSpec((None, Q_BLOCK, ATT_WIDTH), lambda b, i: (b, i, 0))],
        out_specs=pl.BlockSpec((None, Q_BLOCK, ATT_HEADS * HEAD_DIM), lambda b, i: (b, i, 0)),
        out_shape=jax.ShapeDtypeStruct((bsz, s, ATT_HEADS * HEAD_DIM), BF16),
        scratch_shapes=[pltpu.VMEM((s, ATT_KV_HEADS * HEAD_DIM), BF16),
                        pltpu.VMEM((nblk, ATT_KV_HEADS * HEAD_DIM, Q_BLOCK), BF16),
                        pltpu.VMEM((s, LANES), BF16),
                        pltpu.VMEM((s, LANES), I32),
                        pltpu.VMEM((s, LANES), F32),
                        pltpu.VMEM((s, ATT_GROUP * Q_BLOCK), F32),
                        pltpu.VMEM((HEAD_DIM, ATT_GROUP * Q_BLOCK), F32),
                        pltpu.VMEM((1, LANES), I32)],
        compiler_params=_params(("arbitrary", "arbitrary"), 48),
        name="dsa_attention",
    )(positions.reshape(bsz, s, 1), z_att)


def _hgrn_kernel(lbp_ref, gh_ref, q_ref, f_ref, i_ref, og_ref, o_ref, qd_s, ke_s, intra_s, dec_s, *, layer, seq):
    h = pl.program_id(1)
    c = HG_CHUNK
    gr = HG_GROUP_ROWS
    cpg = gr // c
    lbp = lbp_ref[:, pl.ds(h, 1), :]
    e = jnp.exp(lbp - jnp.max(lbp, axis=0, keepdims=True))
    sm = e / jnp.sum(e, axis=0, keepdims=True)
    lb = sm[0]
    for r in range(1, layer + 1):
        lb = lb + sm[r]
    g_head = gh_ref[pl.ds(h, 1), :]

    row = lax.broadcasted_iota(I32, (gr, HG_DIM), 0)
    in_chunk = row % c
    t_io = lax.broadcasted_iota(I32, (gr, gr), 0)
    s_io = lax.broadcasted_iota(I32, (gr, gr), 1)
    att_mask = jnp.logical_and(t_io // c == s_io // c, s_io <= t_io)

    def group_body(g, carry):
        rows = pl.ds(pl.multiple_of(g * gr, gr), gr)
        q = _silu(q_ref[rows, :])
        f = lb + (1.0 - lb) * _sigmoid(f_ref[rows, :])
        k = 1.0 - f
        cum = jnp.log(f)
        shift = 1
        while shift < c:
            cum = cum + jnp.where(in_chunk >= shift, pltpu.roll(cum, shift, 0), 0.0)
            shift *= 2
        cum3 = cum.reshape(cpg, c, HG_DIM)
        cum_last3 = cum3[:, c - 1:c, :]
        q_dec = (q * jnp.exp(cum)).astype(BF16)
        k_dec = (k * jnp.exp(-cum)).astype(BF16)
        att = lax.dot_general(q_dec, k_dec, (((1,), (1,)), ((), ())), preferred_element_type=F32)
        att = jnp.where(att_mask, att, 0.0)
        intra_s[rows, :] = jnp.dot(att.astype(BF16), i_ref[rows, :].astype(BF16), preferred_element_type=F32)
        qd_s[rows, :] = q_dec
        k_end3 = k.reshape(cpg, c, HG_DIM) * jnp.exp(cum_last3 - cum3)
        ke_s[rows, :] = k_end3.reshape(gr, HG_DIM).astype(BF16)
        dec_s[pl.ds(pl.multiple_of(g * cpg, cpg), cpg), :] = jnp.exp(cum_last3).reshape(cpg, HG_DIM)
        return carry

    lax.fori_loop(0, seq // gr, group_body, 0)

    def chunk_body(n, st):
        rows = pl.ds(pl.multiple_of(n * c, c), c)
        inter = lax.dot_general(qd_s[rows, :], st.astype(BF16), (((1,), (1,)), ((), ())),
                                preferred_element_type=F32)
        o = intra_s[rows, :] + inter
        y = o * lax.rsqrt(jnp.mean(o * o, axis=-1, keepdims=True) + EPS) * g_head
        o_ref[rows, :] = (y * _silu(og_ref[rows, :])).astype(o_ref.dtype)
        upd_t = jnp.dot(i_ref[rows, :].T.astype(BF16), ke_s[rows, :], preferred_element_type=F32)
        return dec_s[pl.ds(n, 1), :] * st + upd_t

    lax.fori_loop(0, seq // c, chunk_body, jnp.zeros((HG_DIM, HG_DIM), F32), unroll=8)


def _hgrn(z_hg, lower_bounds, g_head, layer):
    bsz, s, _ = z_hg.shape
    assert s % HG_GROUP_ROWS == 0 and (HG_GROUP_ROWS // HG_CHUNK) % SUBLANES == 0
    depth1 = lower_bounds.shape[0]
    col = lambda sec: (lambda b, h: (b, 0, sec * HG_HEADS + h))
    specs = [pl.BlockSpec((None, s, HG_DIM), col(sec)) for sec in range(4)]
    return pl.pallas_call(
        functools.partial(_hgrn_kernel, layer=layer, seq=s),
        grid=(bsz, HG_HEADS),
        in_specs=[pl.BlockSpec((depth1, HG_HEADS, HG_DIM), lambda b, h: (0, 0, 0)),
                  pl.BlockSpec((HG_HEADS, HG_DIM), lambda b, h: (0, 0))] + specs,
        out_specs=pl.BlockSpec((None, s, HG_DIM), lambda b, h: (b, 0, h)),
        out_shape=jax.ShapeDtypeStruct((bsz, s, HG_HEADS * HG_DIM), BF16),
        scratch_shapes=[pltpu.VMEM((s, HG_DIM), BF16),
                        pltpu.VMEM((s, HG_DIM), BF16),
                        pltpu.VMEM((s, HG_DIM), F32),
                        pltpu.VMEM((s // HG_CHUNK, HG_DIM), F32)],
        compiler_params=_params(("arbitrary", "arbitrary"), 48),
        name="hgrn2",
    )(lower_bounds.reshape(depth1, HG_HEADS, HG_DIM), g_head, z_hg, z_hg, z_hg, z_hg)


def _merge_kernel(ya_ref, yh_ref, ga_ref, gh_ref, x_ref, mod_ref, g2_ref, wa_ref, wh_ref, wo_ref, wr_ref, br_ref,
                  x1_ref, h3_ref, cw3_ref):
    a = jnp.dot(ya_ref[...], wa_ref[...], preferred_element_type=F32)
    hg = jnp.dot(yh_ref[...], wh_ref[...], preferred_element_type=F32)
    merged = _sigmoid(ga_ref[...]) * a + _sigmoid(gh_ref[...]) * hg
    x1 = x_ref[...] + mod_ref[2:3, :] * jnp.dot(merged.astype(BF16), wo_ref[...], preferred_element_type=F32)
    x1_ref[...] = x1
    h2 = _ada_norm(x1, g2_ref[...], mod_ref[3:4, :], mod_ref[4:5, :])
    for t in range(h2.shape[1] // LANES):
        h3_ref[:, t, :] = h2[:, t * LANES:(t + 1) * LANES]

    lg = jnp.dot(h2, wr_ref[...], preferred_element_type=F32, precision=lax.Precision.HIGHEST) + br_ref[...]
    lane = lax.broadcasted_iota(I32, lg.shape, 1)
    neg = -jnp.inf
    is_grp = jnp.logical_and(lane >= N_EXPERTS, lane < N_EXPERTS + N_GROUPS)
    g_lg = jnp.where(is_grp, lg, neg)
    g_max = jnp.max(g_lg, axis=-1, keepdims=True)
    p_grp = 1.0 / jnp.sum(jnp.exp(g_lg - g_max), axis=-1, keepdims=True)
    grp = jnp.min(jnp.where(g_lg == g_max, lane, 2 * LANES), axis=-1, keepdims=True) - N_EXPERTS
    lo = grp * EXPERTS_PER_GROUP
    in_grp = jnp.logical_and(lane >= lo, lane < lo + EXPERTS_PER_GROUP)
    e_lg = jnp.where(in_grp, lg, neg)
    e_max = jnp.max(e_lg, axis=-1, keepdims=True)
    idx1 = jnp.min(jnp.where(e_lg == e_max, lane, 2 * LANES), axis=-1, keepdims=True)
    e_lg2 = jnp.where(lane == idx1, neg, e_lg)
    e_max2 = jnp.max(e_lg2, axis=-1, keepdims=True)
    idx2 = jnp.min(jnp.where(e_lg2 == e_max2, lane, 2 * LANES), axis=-1, keepdims=True)
    e2 = jnp.exp(e_max2 - e_max)
    p1 = p_grp / (1.0 + e2)
    p2 = p_grp * e2 / (1.0 + e2)
    cw = jnp.where(lane == idx1, p1, jnp.where(lane == idx2, p2, jnp.where(lane == grp + N_EXPERTS, 1.0, 0.0)))
    cw3_ref[...] = jnp.zeros_like(cw3_ref)
    cw3_ref[:, 0, :] = cw


def _merge(y_att, y_hg, z_gate, x, mod, g2, wa, wh, wo, wr, br):
    bsz, s, d = x.shape
    tm = min(256, s)
    dh = y_att.shape[-1]
    const = lambda shape: pl.BlockSpec(shape, lambda b, i: (0,) * len(shape))
    row = lambda width, j=0: pl.BlockSpec((None, tm, width), lambda b, i: (b, i, j))
    slab = lambda rows: pl.BlockSpec((tm, rows, LANES), lambda b, i: (b * (s // tm) + i, 0, 0))
    return pl.pallas_call(
        _merge_kernel,
        grid=(bsz, s // tm),
        in_specs=[row(dh), row(dh), row(d, 0), row(d, 1), row(d),
                  pl.BlockSpec((None, 6, d), lambda b, i: (b, 0, 0)), const((1, d)),
                  const((dh, d)), const((dh, d)), const((d, d)), const((d, LANES)), const((1, LANES))],
        out_specs=[row(d), slab(d // LANES), slab(SUBLANES)],
        out_shape=[jax.ShapeDtypeStruct((bsz, s, d), F32),
                   jax.ShapeDtypeStruct((bsz * s, d // LANES, LANES), F32),
                   jax.ShapeDtypeStruct((bsz * s, SUBLANES, LANES), F32)],
        compiler_params=_params(("arbitrary", "arbitrary"), 56),
        name="merge_router",
    )(y_att, y_hg, z_gate, z_gate, x, mod, g2.reshape(1, d), wa, wh, wo, wr, br)


def _moe_kernel(tg_ref, nv_ref, src_ref, h3_hbm, cw3_hbm, wg_ref, wu_ref, wd_ref, y3_hbm,
                hbuf, cbuf, xb, cwv, acc, ybuf, gsem, ssem, *, tm):
    k = pl.program_id(0)
    e = pl.program_id(1)
    nt = pl.num_programs(0)
    ne = pl.num_programs(1)
    slot = k % 2
    nv = nv_ref[k]
    n_slab = xb.shape[1] // LANES

    def gather_copies(tile, slot_, r):
        tok = src_ref[tile * tm + r]
        return (pltpu.make_async_copy(h3_hbm.at[tok], hbuf.at[slot_, r], gsem.at[slot_]),
                pltpu.make_async_copy(cw3_hbm.at[tok], cbuf.at[slot_, r], gsem.at[slot_]))

    def gather(tile, slot_, start):
        def body(r, carry):
            for cp in gather_copies(tile, slot_, r):
                cp.start() if start else cp.wait()
            return carry

        lax.fori_loop(0, tm, body, 0, unroll=8)

    def scatter(tile, start):
        def body(r, carry):
            cp = pltpu.make_async_copy(ybuf.at[r], y3_hbm.at[src_ref[tile * tm + r]], ssem.at[0])
            cp.start() if start else cp.wait()
            return carry

        lax.fori_loop(0, nv_ref[tile], body, 0)

    @pl.when(e == 0)
    def _():
        @pl.when(jnp.logical_and(k == 0, nv > 0))
        def _():
            gather(k, slot, True)

        nxt = jnp.minimum(k + 1, nt - 1)

        @pl.when(jnp.logical_and(k + 1 < nt, nv_ref[nxt] > 0))
        def _():
            gather(nxt, 1 - slot, True)

        @pl.when(nv > 0)
        def _():
            gather(k, slot, False)
            for t in range(n_slab):
                xb[:, t * LANES:(t + 1) * LANES] = hbuf[slot, :, t, :].astype(BF16)
            cwv[...] = cbuf[slot, :, 0, :]
            acc[...] = jnp.zeros_like(acc)

    @pl.when(nv > 0)
    def _():
        h = xb[...]
        a = jnp.dot(h, wg_ref[...].astype(BF16), preferred_element_type=F32)
        u = jnp.dot(h, wu_ref[...].astype(BF16), preferred_element_type=F32)
        cw = cwv[...]
        lane = lax.broadcasted_iota(I32, cw.shape, 1)
        cwe = jnp.sum(jnp.where(lane == tg_ref[k] * ne + e, cw, 0.0), axis=-1, keepdims=True)
        hm = (_silu(a) * u * cwe).astype(BF16)
        acc[...] += jnp.dot(hm, wd_ref[...].astype(BF16), preferred_element_type=F32)

    @pl.when(e == ne - 1)
    def _():
        @pl.when(k > 0)
        def _():
            scatter(jnp.maximum(k - 1, 0), False)

        @pl.when(nv > 0)
        def _():
            for t in range(n_slab):
                ybuf[:, t, :] = acc[:, t * LANES:(t + 1) * LANES]
            scatter(k, True)

        @pl.when(k == nt - 1)
        def _():
            scatter(k, False)


def _moe_tiles(grp, tm):
    t = grp.shape[0]
    nt = t // tm + N_GROUPS
    order = jnp.argsort(grp, stable=True).astype(I32)
    counts = jnp.sum((grp[:, None] == jnp.arange(N_GROUPS, dtype=I32)[None, :]).astype(I32), axis=0)
    tiles_g = (counts + tm - 1) // tm
    tile_end = jnp.cumsum(tiles_g)
    tile0 = tile_end - tiles_g
    gstart = jnp.cumsum(counts) - counts
    k = jnp.arange(nt, dtype=I32)
    used = k < tile_end[-1]
    tg = jnp.sum((k[:, None] >= tile_end[None, :]).astype(I32), axis=1)
    last_g = jnp.sum((tile_end[-1] - 1 >= tile_end).astype(I32))
    tg = jnp.where(used, tg, last_g)
    q0 = (k - tile0[tg]) * tm
    nvalid = jnp.where(used, jnp.clip(counts[tg] - q0, 0, tm), 0).astype(I32)
    r = jnp.arange(tm, dtype=I32)
    idx = jnp.clip(gstart[tg][:, None] + q0[:, None] + r[None, :], 0, t - 1)
    src = jnp.where(r[None, :] < nvalid[:, None], order[idx], 0).astype(I32)
    return tg.astype(I32), nvalid, src.reshape(nt * tm)


def _moe(h3, cw3, wg, wu, wd):
    t, n_slab, _ = h3.shape
    d = n_slab * LANES
    ne_all, _, ff = wg.shape
    tm = min(512, t)
    grp = jnp.argmax(cw3[:, 0, N_EXPERTS:N_EXPERTS + N_GROUPS], axis=-1).astype(I32)
    tg, nvalid, src = _moe_tiles(grp, tm)
    nt = tg.shape[0]
    w_idx = lambda k, e, tg_r, nv_r, src_r: (tg_r[k] * EXPERTS_PER_GROUP + jnp.where(nv_r[k] > 0, e, EXPERTS_PER_GROUP - 1),
                                           0, 0)
    grid_spec = pltpu.PrefetchScalarGridSpec(
        num_scalar_prefetch=3,
        grid=(nt, EXPERTS_PER_GROUP),
        in_specs=[pl.BlockSpec(memory_space=pl.ANY), pl.BlockSpec(memory_space=pl.ANY),
                  pl.BlockSpec((None, d, ff), w_idx), pl.BlockSpec((None, d, ff), w_idx),
                  pl.BlockSpec((None, ff, d), w_idx)],
        out_specs=pl.BlockSpec(memory_space=pl.ANY),
        scratch_shapes=[pltpu.VMEM((2, tm, n_slab, LANES), F32),
                        pltpu.VMEM((2, tm, SUBLANES, LANES), F32),
                        pltpu.VMEM((tm, d), BF16),
                        pltpu.VMEM((tm, LANES), F32),
                        pltpu.VMEM((tm, d), F32),
                        pltpu.VMEM((tm, n_slab, LANES), F32),
                        pltpu.SemaphoreType.DMA((2,)),
                        pltpu.SemaphoreType.DMA((1,))])
    return pl.pallas_call(
        functools.partial(_moe_kernel, tm=tm),
        grid_spec=grid_spec,
        out_shape=jax.ShapeDtypeStruct((t, n_slab, LANES), F32),
        compiler_params=_params(("arbitrary", "arbitrary"), 56),
        name="moe_experts",
    )(tg, nvalid, src, h3, cw3, wg, wu, wd)


def _final_kernel(x1_ref, y3_ref, mod_ref, g_ref, o_ref, *, gate_row):
    y = jnp.concatenate([y3_ref[:, t, :] for t in range(y3_ref.shape[1])], axis=1)
    x2 = x1_ref[...] + mod_ref[gate_row:gate_row + 1, :] * y
    ms = jnp.mean(x2 * x2, axis=-1, keepdims=True)
    o_ref[...] = x2 * lax.rsqrt(ms + EPS) * g_ref[...]


def _final(x1, y3, mod, g):
    bsz, s, d = x1.shape
    tm = min(512, s)
    return pl.pallas_call(
        functools.partial(_final_kernel, gate_row=5),
        grid=(bsz, s // tm),
        in_specs=[pl.BlockSpec((None, tm, d), lambda b, i: (b, i, 0)),
                  pl.BlockSpec((tm, d // LANES, LANES), lambda b, i: (b * (s // tm) + i, 0, 0)),
                  pl.BlockSpec((None, 6, d), lambda b, i: (b, 0, 0)),
                  pl.BlockSpec((1, d), lambda b, i: (0, 0))],
        out_specs=pl.BlockSpec((None, tm, d), lambda b, i: (b, i, 0)),
        out_shape=jax.ShapeDtypeStruct((bsz, s, d), F32),
        compiler_params=_params(("arbitrary", "arbitrary"), 40),
        name="residual_final_norm",
    )(x1, y3, mod, g.reshape(1, d))


def kernel(x, c, positions, w_ada, b_ada, g_norm1, w_in, g_head, hg_lower_bounds, w_attn_up, w_hgrn_up, w_out,
           g_norm2, w_router_group, b_router_group, w_router_expert, b_router_expert, w_exp_gate, w_exp_up,
           w_exp_down, g_final):
    depth = w_ada.shape[0]
    assert depth == 1
    hg_w = HG_HEADS * HG_DIM
    for l in range(depth):
        mod = _modulation(c, w_ada[l], b_ada[l])
        w_l = w_in[l]
        w_att = jnp.pad(w_l[:, :ATT_COLS], ((0, 0), (0, ATT_WIDTH - ATT_COLS))).astype(BF16)
        w_hg = w_l[:, ATT_COLS:ATT_COLS + 4 * hg_w].astype(BF16)
        w_gate = w_l[:, ATT_COLS + 4 * hg_w:].astype(BF16)
        z_att = _norm_matmul(x, mod, g_norm1[l], w_att, ATT_WIDTH)
        z_hg = _norm_matmul(x, mod, g_norm1[l], w_hg, 1024)
        z_gate = _norm_matmul(x, mod, g_norm1[l], w_gate, 1024)
        y_att = _attention(z_att, positions)
        y_hg = _hgrn(z_hg, hg_lower_bounds, g_head[l], l)
        w_r = jnp.pad(jnp.concatenate([w_router_expert[l], w_router_group[l]], axis=1),
                      ((0, 0), (0, LANES - N_EXPERTS - N_GROUPS)))
        b_r = jnp.pad(jnp.concatenate([b_router_expert[l], b_router_group[l]]),
                      (0, LANES - N_EXPERTS - N_GROUPS)).reshape(1, LANES)
        x1, h3, cw3 = _merge(y_att, y_hg, z_gate, x, mod, g_norm2[l],
                             w_attn_up[l].astype(BF16), w_hgrn_up[l].astype(BF16), w_out[l].astype(BF16), w_r, b_r)
        y3 = _moe(h3, cw3, w_exp_gate[l], w_exp_up[l], w_exp_down[l])
        x = _final(x1, y3, mod, g_final)
    return x
```

```python
import functools
import math

import jax
import jax.numpy as jnp
from jax import lax
from jax.experimental import pallas as pl
from jax.experimental.pallas import tpu as pltpu

F32 = jnp.float32
BF16 = jnp.bfloat16
I32 = jnp.int32

ATT_HEADS = 8
ATT_KV_HEADS = 2
ATT_GROUP = ATT_HEADS // ATT_KV_HEADS
HEAD_DIM = 128
IDX_HEADS = 8
IDX_DIM = 64
TOPK_MAX = 256
Q_BLOCK = 128
KEY_CHUNK = 512
ROPE_THETA = 10000.0
ATT_SCALE = HEAD_DIM ** -0.5
IDX_SCALE = (IDX_HEADS * IDX_DIM) ** -0.5
HG_HEADS = 8
HG_DIM = 128
HG_CHUNK = 64
HG_GROUP_ROWS = 512
N_GROUPS = 4
EXPERTS_PER_GROUP = 8
N_EXPERTS = N_GROUPS * EXPERTS_PER_GROUP
EPS = 1e-6

LANES = 128
SUBLANES = 8
INT_MIN = -(2 ** 31)
NEG_INF_KEY = INT_MIN + 0x7FFFFF
MIB = 1024 * 1024

AQ_OFF = 0
AK_OFF = ATT_HEADS * HEAD_DIM
AV_OFF = AK_OFF + ATT_KV_HEADS * HEAD_DIM
IQ_OFF = AV_OFF + ATT_KV_HEADS * HEAD_DIM
IK_OFF = IQ_OFF + IDX_HEADS * IDX_DIM
ATT_COLS = IK_OFF + IDX_DIM + IDX_HEADS
ATT_WIDTH = IK_OFF + LANES


def _sigmoid(x):
    return 1.0 / (1.0 + jnp.exp(-x))


def _silu(x):
    return x * _sigmoid(x)


def _params(sem, vmem_mib):
    return pltpu.CompilerParams(dimension_semantics=sem, vmem_limit_bytes=vmem_mib * MIB)


def _mod_kernel(c_ref, w_ref, b_ref, o_ref):
    c = c_ref[...]
    ca = _silu(c).astype(BF16)
    o_ref[...] = jnp.dot(ca, w_ref[...].astype(BF16), preferred_element_type=F32) + b_ref[...]


def _modulation(c, w, b):
    bsz, d = c.shape
    n = w.shape[1]
    rows = 8
    cp = jnp.zeros((rows, d), F32).at[:bsz].set(c)
    tn = 1024
    out = pl.pallas_call(
        _mod_kernel,
        grid=(n // tn,),
        in_specs=[pl.BlockSpec((rows, d), lambda j: (0, 0)),
                  pl.BlockSpec((d, tn), lambda j: (0, j)),
                  pl.BlockSpec((1, tn), lambda j: (0, j))],
        out_specs=pl.BlockSpec((rows, tn), lambda j: (0, j)),
        out_shape=jax.ShapeDtypeStruct((rows, n), F32),
        compiler_params=_params(("arbitrary",), 40),
        name="modulation",
    )(cp, w, b.reshape(1, n))
    return out[:bsz].reshape(bsz, 6, d)


def _w_head_kernel(w_ref, o_ref, *, n_valid):
    w = w_ref[...]
    lane = lax.broadcasted_iota(I32, w.shape, 1)
    o_ref[...] = jnp.where(lane < n_valid, w, 0.0).astype(BF16)


def _w_shift_kernel(a_ref, b_ref, o_ref, *, shift):
    ab = jnp.concatenate([a_ref[...], b_ref[...]], axis=1)
    o_ref[...] = ab[:, shift:shift + o_ref.shape[1]].astype(BF16)


def _prep_w_in(w):
    d, n = w.shape
    tr = min(512, d)
    tn = 1024
    base = (ATT_COLS // LANES) * LANES
    rest = n - ATT_COLS
    assert base % tn == 0 and rest % tn == 0 and ATT_WIDTH <= n
    w_att = pl.pallas_call(
        functools.partial(_w_head_kernel, n_valid=ATT_COLS),
        grid=(d // tr,),
        in_specs=[pl.BlockSpec((tr, ATT_WIDTH), lambda r: (r, 0))],
        out_specs=pl.BlockSpec((tr, ATT_WIDTH), lambda r: (r, 0)),
        out_shape=jax.ShapeDtypeStruct((d, ATT_WIDTH), BF16),
        compiler_params=_params(("arbitrary",), 40),
        name="w_in_head",
    )(w)
    w_rest = pl.pallas_call(
        functools.partial(_w_shift_kernel, shift=ATT_COLS - base),
        grid=(d // tr, rest // tn),
        in_specs=[pl.BlockSpec((tr, tn), lambda r, j: (r, base // tn + j)),
                  pl.BlockSpec((tr, LANES), lambda r, j: (r, (base + (j + 1) * tn) // LANES))],
        out_specs=pl.BlockSpec((tr, tn), lambda r, j: (r, j)),
        out_shape=jax.ShapeDtypeStruct((d, rest), BF16),
        compiler_params=_params(("arbitrary", "arbitrary"), 40),
        name="w_in_rest",
    )(w, w)
    return w_att, w_rest


def _ada_norm(x, g, shift, scale):
    ms = jnp.mean(x * x, axis=-1, keepdims=True)
    y = x * lax.rsqrt(ms + EPS) * g
    return y * (1.0 + scale) + shift


def _norm_matmul_kernel(x_ref, mod_ref, g_ref, w_ref, o_ref, h_ref):
    @pl.when(pl.program_id(2) == 0)
    def _():
        h = _ada_norm(x_ref[...], g_ref[...], mod_ref[0:1, :], mod_ref[1:2, :])
        h_ref[...] = h.astype(BF16)

    o_ref[...] = jnp.dot(h_ref[...], w_ref[...], preferred_element_type=F32)


def _norm_matmul(x, mod, g, w, tn):
    bsz, s, d = x.shape
    n = w.shape[1]
    tm = min(1024 if tn <= 1024 else 512, s)
    return pl.pallas_call(
        _norm_matmul_kernel,
        grid=(bsz, s // tm, n // tn),
        in_specs=[pl.BlockSpec((None, tm, d), lambda b, i, j: (b, i, 0)),
                  pl.BlockSpec((None, 6, d), lambda b, i, j: (b, 0, 0)),
                  pl.BlockSpec((1, d), lambda b, i, j: (0, 0)),
                  pl.BlockSpec((d, tn), lambda b, i, j: (0, j))],
        out_specs=pl.BlockSpec((None, tm, tn), lambda b, i, j: (b, i, j)),
        out_shape=jax.ShapeDtypeStruct((bsz, s, n), F32),
        scratch_shapes=[pltpu.VMEM((tm, d), BF16)],
        compiler_params=_params(("arbitrary", "arbitrary", "arbitrary"), 48),
        name="norm_matmul",
    )(x, mod, g.reshape(1, d), w)


def _sortable(score):
    bits = pltpu.bitcast(score + 0.0, I32)
    return jnp.where(bits < 0, bits ^ 0x7FFFFFFF, bits)


def _attn_kernel(pos_ref, z_ref, o_ref, k_s, vt_s, ik_s, key_s, bias_s, lg_s, acc_s, thr_s, *, topk):
    i = pl.program_id(1)
    qb = Q_BLOCK
    kc = KEY_CHUNK
    n_chunks = (i * qb + kc) // kc
    r0 = pl.multiple_of(i * qb, qb)

    @pl.when(jnp.logical_and(pl.program_id(0) == 0, i == 0))
    def _():
        k_s[...] = jnp.zeros_like(k_s)
        vt_s[...] = jnp.zeros_like(vt_s)
        ik_s[...] = jnp.zeros_like(ik_s)

    lane = lax.broadcasted_iota(I32, (1, LANES), 1)
    pos = pos_ref[...].astype(F32)
    inv_a = jnp.exp((lane % (HEAD_DIM // 2)).astype(F32) * (-2.0 * math.log(ROPE_THETA) / HEAD_DIM))
    ang_a = pos * inv_a
    cos_a = jnp.cos(ang_a)
    sin_a = jnp.where(lane < HEAD_DIM // 2, -1.0, 1.0) * jnp.sin(ang_a)
    inv_i = jnp.exp((lane % (IDX_DIM // 2)).astype(F32) * (-2.0 * math.log(ROPE_THETA) / IDX_DIM))
    ang_i = pos * inv_i
    low_i = (lane % IDX_DIM) < IDX_DIM // 2
    cos_i = jnp.cos(ang_i)
    sin_i = jnp.where(low_i, -1.0, 1.0) * jnp.sin(ang_i)

    def rope_a(x):
        return x * cos_a + pltpu.roll(x, HEAD_DIM // 2, 1) * sin_a

    def rope_i(x):
        partner = jnp.where(low_i, pltpu.roll(x, LANES - IDX_DIM // 2, 1), pltpu.roll(x, IDX_DIM // 2, 1))
        return x * cos_i + partner * sin_i

    for h in range(ATT_KV_HEADS):
        kh = rope_a(z_ref[:, AK_OFF + h * HEAD_DIM:AK_OFF + (h + 1) * HEAD_DIM])
        k_s[pl.ds(r0, qb), h * HEAD_DIM:(h + 1) * HEAD_DIM] = kh.astype(BF16)
    v = z_ref[:, AV_OFF:AV_OFF + ATT_KV_HEADS * HEAD_DIM]
    vt_s[i] = v.T.astype(BF16)
    tile = z_ref[:, IK_OFF:IK_OFF + LANES]
    ikr = rope_i(tile)
    ik2 = jnp.where(lane < IDX_DIM, ikr, pltpu.roll(ikr, IDX_DIM, 1))
    ik_s[pl.ds(r0, qb), :] = ik2.astype(BF16)
    w_t = tile.T[IDX_DIM:IDX_DIM + IDX_HEADS, :] * IDX_SCALE

    q_rows = []
    for t in range(IDX_HEADS // 2):
        iqr = rope_i(z_ref[:, IQ_OFF + t * LANES:IQ_OFF + (t + 1) * LANES])
        q_rows.append(jnp.where(lane < IDX_DIM, iqr, 0.0).astype(BF16))
        q_rows.append(jnp.where(lane < IDX_DIM, 0.0, iqr).astype(BF16))
    iq_all = jnp.concatenate(q_rows, axis=0)

    key_iota = lax.broadcasted_iota(I32, (kc, LANES), 0)
    q_idx = i * qb + lane

    def chunk_rows(j):
        return pl.ds(pl.multiple_of(j * kc, kc), kc)

    zero8 = jnp.zeros((SUBLANES, LANES), I32)

    def count8(mask):
        return jnp.sum(jnp.where(mask, 1, 0).reshape(kc // SUBLANES, SUBLANES, LANES), axis=0)

    def score_body(j, carry):
        dots = lax.dot_general(ik_s[chunk_rows(j), :], iq_all, (((1,), (1,)), ((), ())),
                               preferred_element_type=F32)
        sc = jnp.zeros((kc, LANES), F32)
        for h in range(IDX_HEADS):
            sc = sc + jnp.maximum(dots[:, h * qb:(h + 1) * qb], 0.0) * w_t[h:h + 1, :]
        sc = jnp.where(key_iota + j * kc <= q_idx, sc, -jnp.inf)
        key_s[chunk_rows(j), :] = _sortable(sc)
        return carry

    lax.fori_loop(0, n_chunks, score_body, 0)

    thr_s[...] = jnp.full((1, LANES), NEG_INF_KEY + 1, I32)
    select = i >= topk // qb

    @pl.when(select)
    def _():
        def bit_body(b, t_u):
            cand = t_u | jnp.left_shift(jnp.int32(1), 31 - b)
            cand_s = cand ^ INT_MIN

            def cnt_body(j, acc):
                return acc + count8(key_s[chunk_rows(j), :] >= cand_s)

            cnt = jnp.sum(lax.fori_loop(0, n_chunks, cnt_body, zero8), axis=0, keepdims=True)
            return jnp.where(cnt >= topk, cand, t_u)

        t_u = lax.fori_loop(0, 32, bit_body, jnp.zeros((1, LANES), I32))
        thr_s[...] = t_u ^ INT_MIN

    thr = thr_s[...]

    def bias_body(j, cnt):
        sel = key_s[chunk_rows(j), :] >= thr
        bias_s[chunk_rows(j), :] = jnp.where(sel, 0.0, -jnp.inf)
        return cnt + count8(sel)

    cnt_ge = jnp.sum(lax.fori_loop(0, n_chunks, bias_body, zero8), axis=0, keepdims=True)
    has_ties = jnp.logical_and(select, jnp.max(cnt_ge) > topk)

    @pl.when(has_ties)
    def _():
        def eq_body(j, acc):
            return acc + count8(key_s[chunk_rows(j), :] == thr)

        cnt_eq = jnp.sum(lax.fori_loop(0, n_chunks, eq_body, zero8), axis=0, keepdims=True)
        n_keep = (topk - (cnt_ge - cnt_eq)).astype(F32)
        tril = (lax.broadcasted_iota(I32, (kc, kc), 1) <= lax.broadcasted_iota(I32, (kc, kc), 0)).astype(BF16)

        def tie_body(j, run):
            kk = key_s[chunk_rows(j), :]
            eq = kk == thr
            pref = jnp.dot(tril, jnp.where(eq, 1.0, 0.0).astype(BF16), preferred_element_type=F32) + run
            sel = jnp.logical_or(kk > thr, jnp.logical_and(eq, pref <= n_keep))
            bias_s[chunk_rows(j), :] = jnp.where(sel, 0.0, -jnp.inf)
            return pref[kc - 1:kc, :]

        lax.fori_loop(0, n_chunks, tie_body, jnp.zeros((1, LANES), F32))

    gw = ATT_GROUP * qb
    for h in range(ATT_KV_HEADS):
        qs = jnp.concatenate(
            [(rope_a(z_ref[:, AQ_OFF + (h * ATT_GROUP + g) * HEAD_DIM:AQ_OFF + (h * ATT_GROUP + g + 1) * HEAD_DIM])
              * ATT_SCALE).astype(BF16) for g in range(ATT_GROUP)], axis=0)

        def logit_body(j, m8):
            kch = k_s[chunk_rows(j), h * HEAD_DIM:(h + 1) * HEAD_DIM]
            lt = lax.dot_general(kch, qs, (((1,), (1,)), ((), ())), preferred_element_type=F32)
            bias = bias_s[chunk_rows(j), :]
            lm = lt + jnp.concatenate([bias] * ATT_GROUP, axis=1)
            lg_s[chunk_rows(j), :] = lm
            return jnp.maximum(m8, jnp.max(lm.reshape(kc // 8, 8, gw), axis=0))

        m8 = lax.fori_loop(0, n_chunks, logit_body, jnp.full((8, gw), -jnp.inf, F32))
        m = jnp.max(m8, axis=0, keepdims=True)
        acc_s[...] = jnp.zeros_like(acc_s)

        def pv_body(j, l8):
            p = jnp.exp(lg_s[chunk_rows(j), :] - m)
            vt = jnp.concatenate([vt_s[(kc // qb) * j + t][h * HEAD_DIM:(h + 1) * HEAD_DIM, :]
                                  for t in range(kc // qb)], axis=1)
            acc_s[...] += jnp.dot(vt, p.astype(BF16), preferred_element_type=F32)
            return l8 + jnp.sum(p.reshape(kc // 8, 8, gw), axis=0)

        l8 = lax.fori_loop(0, n_chunks, pv_body, jnp.zeros((8, gw), F32))
        o_t = acc_s[...] / jnp.sum(l8, axis=0, keepdims=True)
        for g in range(ATT_GROUP):
            c0 = (h * ATT_GROUP + g) * HEAD_DIM
            o_ref[:, c0:c0 + HEAD_DIM] = o_t[:, g * qb:(g + 1) * qb].T.astype(o_ref.dtype)


def _attention(z_att, positions):
    bsz, s, _ = z_att.shape
    nblk = s // Q_BLOCK
    topk = min(TOPK_MAX, s // 4)
    assert s % KEY_CHUNK == 0 and KEY_CHUNK % Q_BLOCK == 0 and topk % Q_BLOCK == 0
    return pl.pallas_call(
        functools.partial(_attn_kernel, topk=topk),
        grid=(bsz, nblk),
        in_specs=[pl.BlockSpec((None, Q_BLOCK, 1), lambda b, i: (b, i, 0)),
                  pl.BlockSpec((None, Q_BLOCK, ATT_WIDTH), lambda b, i: (b, i, 0))],
        out_specs=pl.BlockSpec((None, Q_BLOCK, ATT_HEADS * HEAD_DIM), lambda b, i: (b, i, 0)),
        out_shape=jax.ShapeDtypeStruct((bsz, s, ATT_HEADS * HEAD_DIM), BF16),
        scratch_shapes=[pltpu.VMEM((s, ATT_KV_HEADS * HEAD_DIM), BF16),
                        pltpu.VMEM((nblk, ATT_KV_HEADS * HEAD_DIM, Q_BLOCK), BF16),
                        pltpu.VMEM((s, LANES), BF16),
                        pltpu.VMEM((s, LANES), I32),
                        pltpu.VMEM((s, LANES), F32),
                        pltpu.VMEM((s, ATT_GROUP * Q_BLOCK), F32),
                        pltpu.VMEM((HEAD_DIM, ATT_GROUP * Q_BLOCK), F32),
                        pltpu.VMEM((1, LANES), I32)],
        compiler_params=_params(("arbitrary", "arbitrary"), 48),
        name="dsa_attention",
    )(positions.reshape(bsz, s, 1), z_att)


def _hgrn_kernel(lbp_ref, gh_ref, q_ref, f_ref, i_ref, og_ref, o_ref, qd_s, ke_s, intra_s, dec_s, *, layer, seq):
    h = pl.program_id(1)
    c = HG_CHUNK
    gr = HG_GROUP_ROWS
    cpg = gr // c
    lbp = lbp_ref[:, pl.ds(h, 1), :]
    e = jnp.exp(lbp - jnp.max(lbp, axis=0, keepdims=True))
    sm = e / jnp.sum(e, axis=0, keepdims=True)
    lb = sm[0]
    for r in range(1, layer + 1):
        lb = lb + sm[r]
    g_head = gh_ref[pl.ds(h, 1), :]

    row = lax.broadcasted_iota(I32, (gr, HG_DIM), 0)
    in_chunk = row % c
    t_io = lax.broadcasted_iota(I32, (gr, gr), 0)
    s_io = lax.broadcasted_iota(I32, (gr, gr), 1)
    att_mask = jnp.logical_and(t_io // c == s_io // c, s_io <= t_io)

    def group_body(g, carry):
        rows = pl.ds(pl.multiple_of(g * gr, gr), gr)
        q = _silu(q_ref[rows, :])
        f = lb + (1.0 - lb) * _sigmoid(f_ref[rows, :])
        k = 1.0 - f
        cum = jnp.log(f)
        shift = 1
        while shift < c:
            cum = cum + jnp.where(in_chunk >= shift, pltpu.roll(cum, shift, 0), 0.0)
            shift *= 2
        cum3 = cum.reshape(cpg, c, HG_DIM)
        cum_last3 = cum3[:, c - 1:c, :]
        q_dec = (q * jnp.exp(cum)).astype(BF16)
        k_dec = (k * jnp.exp(-cum)).astype(BF16)
        att = lax.dot_general(q_dec, k_dec, (((1,), (1,)), ((), ())), preferred_element_type=F32)
        att = jnp.where(att_mask, att, 0.0)
        intra_s[rows, :] = jnp.dot(att.astype(BF16), i_ref[rows, :].astype(BF16), preferred_element_type=F32)
        qd_s[rows, :] = q_dec
        k_end3 = k.reshape(cpg, c, HG_DIM) * jnp.exp(cum_last3 - cum3)
        ke_s[rows, :] = k_end3.reshape(gr, HG_DIM).astype(BF16)
        dec_s[pl.ds(pl.multiple_of(g * cpg, cpg), cpg), :] = jnp.exp(cum_last3).reshape(cpg, HG_DIM)
        return carry

    lax.fori_loop(0, seq // gr, group_body, 0)

    def chunk_body(n, st):
        rows = pl.ds(pl.multiple_of(n * c, c), c)
        inter = lax.dot_general(qd_s[rows, :], st.astype(BF16), (((1,), (1,)), ((), ())),
                                preferred_element_type=F32)
        o = intra_s[rows, :] + inter
        y = o * lax.rsqrt(jnp.mean(o * o, axis=-1, keepdims=True) + EPS) * g_head
        o_ref[rows, :] = (y * _silu(og_ref[rows, :])).astype(o_ref.dtype)
        upd_t = jnp.dot(i_ref[rows, :].T.astype(BF16), ke_s[rows, :], preferred_element_type=F32)
        return dec_s[pl.ds(n, 1), :] * st + upd_t

    lax.fori_loop(0, seq // c, chunk_body, jnp.zeros((HG_DIM, HG_DIM), F32), unroll=8)


def _hgrn(z_hg, lower_bounds, g_head, layer):
    bsz, s, _ = z_hg.shape
    assert s % HG_GROUP_ROWS == 0 and (HG_GROUP_ROWS // HG_CHUNK) % SUBLANES == 0
    depth1 = lower_bounds.shape[0]
    col = lambda sec: (lambda b, h: (b, 0, sec * HG_HEADS + h))
    specs = [pl.BlockSpec((None, s, HG_DIM), col(sec)) for sec in range(4)]
    return pl.pallas_call(
        functools.partial(_hgrn_kernel, layer=layer, seq=s),
        grid=(bsz, HG_HEADS),
        in_specs=[pl.BlockSpec((depth1, HG_HEADS, HG_DIM), lambda b, h: (0, 0, 0)),
                  pl.BlockSpec((HG_HEADS, HG_DIM), lambda b, h: (0, 0))] + specs,
        out_specs=pl.BlockSpec((None, s, HG_DIM), lambda b, h: (b, 0, h)),
        out_shape=jax.ShapeDtypeStruct((bsz, s, HG_HEADS * HG_DIM), BF16),
        scratch_shapes=[pltpu.VMEM((s, HG_DIM), BF16),
                        pltpu.VMEM((s, HG_DIM), BF16),
                        pltpu.VMEM((s, HG_DIM), F32),
                        pltpu.VMEM((s // HG_CHUNK, HG_DIM), F32)],
        compiler_params=_params(("arbitrary", "arbitrary"), 48),
        name="hgrn2",
    )(lower_bounds.reshape(depth1, HG_HEADS, HG_DIM), g_head, z_hg, z_hg, z_hg, z_hg)


def _merge_kernel(ya_ref, yh_ref, ga_ref, gh_ref, x_ref, mod_ref, g2_ref, wa_ref, wh_ref, wo_ref, wr_ref, br_ref,
                  x1_ref, h3_ref, cw3_ref):
    a = jnp.dot(ya_ref[...], wa_ref[...], preferred_element_type=F32)
    hg = jnp.dot(yh_ref[...], wh_ref[...], preferred_element_type=F32)
    merged = _sigmoid(ga_ref[...]) * a + _sigmoid(gh_ref[...]) * hg
    x1 = x_ref[...] + mod_ref[2:3, :] * jnp.dot(merged.astype(BF16), wo_ref[...], preferred_element_type=F32)
    x1_ref[...] = x1
    h2 = _ada_norm(x1, g2_ref[...], mod_ref[3:4, :], mod_ref[4:5, :])
    for t in range(h2.shape[1] // LANES):
        h3_ref[:, t, :] = h2[:, t * LANES:(t + 1) * LANES]

    lg = jnp.dot(h2, wr_ref[...], preferred_element_type=F32, precision=lax.Precision.HIGHEST) + br_ref[...]
    lane = lax.broadcasted_iota(I32, lg.shape, 1)
    neg = -jnp.inf
    is_grp = jnp.logical_and(lane >= N_EXPERTS, lane < N_EXPERTS + N_GROUPS)
    g_lg = jnp.where(is_grp, lg, neg)
    g_max = jnp.max(g_lg, axis=-1, keepdims=True)
    p_grp = 1.0 / jnp.sum(jnp.exp(g_lg - g_max), axis=-1, keepdims=True)
    grp = jnp.min(jnp.where(g_lg == g_max, lane, 2 * LANES), axis=-1, keepdims=True) - N_EXPERTS
    lo = grp * EXPERTS_PER_GROUP
    in_grp = jnp.logical_and(lane >= lo, lane < lo + EXPERTS_PER_GROUP)
    e_lg = jnp.where(in_grp, lg, neg)
    e_max = jnp.max(e_lg, axis=-1, keepdims=True)
    idx1 = jnp.min(jnp.where(e_lg == e_max, lane, 2 * LANES), axis=-1, keepdims=True)
    e_lg2 = jnp.where(lane == idx1, neg, e_lg)
    e_max2 = jnp.max(e_lg2, axis=-1, keepdims=True)
    idx2 = jnp.min(jnp.where(e_lg2 == e_max2, lane, 2 * LANES), axis=-1, keepdims=True)
    e2 = jnp.exp(e_max2 - e_max)
    p1 = p_grp / (1.0 + e2)
    p2 = p_grp * e2 / (1.0 + e2)
    cw = jnp.where(lane == idx1, p1, jnp.where(lane == idx2, p2, jnp.where(lane == grp + N_EXPERTS, 1.0, 0.0)))
    cw3_ref[...] = jnp.zeros_like(cw3_ref)
    cw3_ref[:, 0, :] = cw


def _merge(y_att, y_hg, z_gate, gate_blk, x, mod, g2, wa, wh, wo, wr, br):
    bsz, s, d = x.shape
    tm = min(256, s)
    dh = y_att.shape[-1]
    const = lambda shape: pl.BlockSpec(shape, lambda b, i: (0,) * len(shape))
    row = lambda width, j=0: pl.BlockSpec((None, tm, width), lambda b, i: (b, i, j))
    slab = lambda rows: pl.BlockSpec((tm, rows, LANES), lambda b, i: (b * (s // tm) + i, 0, 0))
    return pl.pallas_call(
        _merge_kernel,
        grid=(bsz, s // tm),
        in_specs=[row(dh), row(dh), row(d, gate_blk), row(d, gate_blk + 1), row(d),
                  pl.BlockSpec((None, 6, d), lambda b, i: (b, 0, 0)), const((1, d)),
                  const((dh, d)), const((dh, d)), const((d, d)), const((d, LANES)), const((1, LANES))],
        out_specs=[row(d), slab(d // LANES), slab(SUBLANES)],
        out_shape=[jax.ShapeDtypeStruct((bsz, s, d), F32),
                   jax.ShapeDtypeStruct((bsz * s, d // LANES, LANES), F32),
                   jax.ShapeDtypeStruct((bsz * s, SUBLANES, LANES), F32)],
        compiler_params=_params(("arbitrary", "arbitrary"), 56),
        name="merge_router",
    )(y_att, y_hg, z_gate, z_gate, x, mod, g2.reshape(1, d), wa, wh, wo, wr, br)


def _moe_kernel(tg_ref, nv_ref, src_ref, h3_hbm, cw3_hbm, wg_ref, wu_ref, wd_ref, y3_hbm,
                hbuf, cbuf, xb, cwv, acc, ybuf, gsem, ssem, *, tm):
    k = pl.program_id(0)
    e = pl.program_id(1)
    nt = pl.num_programs(0)
    ne = pl.num_programs(1)
    slot = k % 2
    nv = nv_ref[k]
    n_slab = xb.shape[1] // LANES

    def gather_copies(tile, slot_, r):
        tok = src_ref[tile * tm + r]
        return (pltpu.make_async_copy(h3_hbm.at[tok], hbuf.at[slot_, r], gsem.at[slot_]),
                pltpu.make_async_copy(cw3_hbm.at[tok], cbuf.at[slot_, r], gsem.at[slot_]))

    def gather(tile, slot_, start):
        def body(r, carry):
            for cp in gather_copies(tile, slot_, r):
                cp.start() if start else cp.wait()
            return carry

        lax.fori_loop(0, tm, body, 0, unroll=8)

    def scatter(tile, start):
        def body(r, carry):
            cp = pltpu.make_async_copy(ybuf.at[r], y3_hbm.at[src_ref[tile * tm + r]], ssem.at[0])
            cp.start() if start else cp.wait()
            return carry

        lax.fori_loop(0, nv_ref[tile], body, 0)

    @pl.when(e == 0)
    def _():
        @pl.when(jnp.logical_and(k == 0, nv > 0))
        def _():
            gather(k, slot, True)

        nxt = jnp.minimum(k + 1, nt - 1)

        @pl.when(jnp.logical_and(k + 1 < nt, nv_ref[nxt] > 0))
        def _():
            gather(nxt, 1 - slot, True)

        @pl.when(nv > 0)
        def _():
            gather(k, slot, False)
            for t in range(n_slab):
                xb[:, t * LANES:(t + 1) * LANES] = hbuf[slot, :, t, :].astype(BF16)
            cwv[...] = cbuf[slot, :, 0, :]
            acc[...] = jnp.zeros_like(acc)

    @pl.when(nv > 0)
    def _():
        h = xb[...]
        a = jnp.dot(h, wg_ref[...].astype(BF16), preferred_element_type=F32)
        u = jnp.dot(h, wu_ref[...].astype(BF16), preferred_element_type=F32)
        cw = cwv[...]
        lane = lax.broadcasted_iota(I32, cw.shape, 1)
        cwe = jnp.sum(jnp.where(lane == tg_ref[k] * ne + e, cw, 0.0), axis=-1, keepdims=True)
        hm = (_silu(a) * u * cwe).astype(BF16)
        acc[...] += jnp.dot(hm, wd_ref[...].astype(BF16), preferred_element_type=F32)

    @pl.when(e == ne - 1)
    def _():
        @pl.when(k > 0)
        def _():
            scatter(jnp.maximum(k - 1, 0), False)

        @pl.when(nv > 0)
        def _():
            for t in range(n_slab):
                ybuf[:, t, :] = acc[:, t * LANES:(t + 1) * LANES]
            scatter(k, True)

        @pl.when(k == nt - 1)
        def _():
            scatter(k, False)


def _moe_tiles(grp, tm):
    t = grp.shape[0]
    nt = t // tm + N_GROUPS
    order = jnp.argsort(grp, stable=True).astype(I32)
    counts = jnp.sum((grp[:, None] == jnp.arange(N_GROUPS, dtype=I32)[None, :]).astype(I32), axis=0)
    tiles_g = (counts + tm - 1) // tm
    tile_end = jnp.cumsum(tiles_g)
    tile0 = tile_end - tiles_g
    gstart = jnp.cumsum(counts) - counts
    k = jnp.arange(nt, dtype=I32)
    used = k < tile_end[-1]
    tg = jnp.sum((k[:, None] >= tile_end[None, :]).astype(I32), axis=1)
    last_g = jnp.sum((tile_end[-1] - 1 >= tile_end).astype(I32))
    tg = jnp.where(used, tg, last_g)
    q0 = (k - tile0[tg]) * tm
    nvalid = jnp.where(used, jnp.clip(counts[tg] - q0, 0, tm), 0).astype(I32)
    r = jnp.arange(tm, dtype=I32)
    idx = jnp.clip(gstart[tg][:, None] + q0[:, None] + r[None, :], 0, t - 1)
    src = jnp.where(r[None, :] < nvalid[:, None], order[idx], 0).astype(I32)
    return tg.astype(I32), nvalid, src.reshape(nt * tm)


def _moe(h3, cw3, wg, wu, wd):
    t, n_slab, _ = h3.shape
    d = n_slab * LANES
    ne_all, _, ff = wg.shape
    tm = min(512, t)
    grp = jnp.argmax(cw3[:, 0, N_EXPERTS:N_EXPERTS + N_GROUPS], axis=-1).astype(I32)
    tg, nvalid, src = _moe_tiles(grp, tm)
    nt = tg.shape[0]
    w_idx = lambda k, e, tg_r, nv_r, src_r: (tg_r[k] * EXPERTS_PER_GROUP + jnp.where(nv_r[k] > 0, e, EXPERTS_PER_GROUP - 1),
                                           0, 0)
    grid_spec = pltpu.PrefetchScalarGridSpec(
        num_scalar_prefetch=3,
        grid=(nt, EXPERTS_PER_GROUP),
        in_specs=[pl.BlockSpec(memory_space=pl.ANY), pl.BlockSpec(memory_space=pl.ANY),
                  pl.BlockSpec((None, d, ff), w_idx), pl.BlockSpec((None, d, ff), w_idx),
                  pl.BlockSpec((None, ff, d), w_idx)],
        out_specs=pl.BlockSpec(memory_space=pl.ANY),
        scratch_shapes=[pltpu.VMEM((2, tm, n_slab, LANES), F32),
                        pltpu.VMEM((2, tm, SUBLANES, LANES), F32),
                        pltpu.VMEM((tm, d), BF16),
                        pltpu.VMEM((tm, LANES), F32),
                        pltpu.VMEM((tm, d), F32),
                        pltpu.VMEM((tm, n_slab, LANES), F32),
                        pltpu.SemaphoreType.DMA((2,)),
                        pltpu.SemaphoreType.DMA((1,))])
    return pl.pallas_call(
        functools.partial(_moe_kernel, tm=tm),
        grid_spec=grid_spec,
        out_shape=jax.ShapeDtypeStruct((t, n_slab, LANES), F32),
        compiler_params=_params(("arbitrary", "arbitrary"), 56),
        name="moe_experts",
    )(tg, nvalid, src, h3, cw3, wg, wu, wd)


def _final_kernel(x1_ref, y3_ref, mod_ref, g_ref, o_ref, *, gate_row):
    y = jnp.concatenate([y3_ref[:, t, :] for t in range(y3_ref.shape[1])], axis=1)
    x2 = x1_ref[...] + mod_ref[gate_row:gate_row + 1, :] * y
    ms = jnp.mean(x2 * x2, axis=-1, keepdims=True)
    o_ref[...] = x2 * lax.rsqrt(ms + EPS) * g_ref[...]


def _final(x1, y3, mod, g):
    bsz, s, d = x1.shape
    tm = min(512, s)
    return pl.pallas_call(
        functools.partial(_final_kernel, gate_row=5),
        grid=(bsz, s // tm),
        in_specs=[pl.BlockSpec((None, tm, d), lambda b, i: (b, i, 0)),
                  pl.BlockSpec((tm, d // LANES, LANES), lambda b, i: (b * (s // tm) + i, 0, 0)),
                  pl.BlockSpec((None, 6, d), lambda b, i: (b, 0, 0)),
                  pl.BlockSpec((1, d), lambda b, i: (0, 0))],
        out_specs=pl.BlockSpec((None, tm, d), lambda b, i: (b, i, 0)),
        out_shape=jax.ShapeDtypeStruct((bsz, s, d), F32),
        compiler_params=_params(("arbitrary", "arbitrary"), 40),
        name="residual_final_norm",
    )(x1, y3, mod, g.reshape(1, d))


def kernel(x, c, positions, w_ada, b_ada, g_norm1, w_in, g_head, hg_lower_bounds, w_attn_up, w_hgrn_up, w_out,
           g_norm2, w_router_group, b_router_group, w_router_expert, b_router_expert, w_exp_gate, w_exp_up,
           w_exp_down, g_final):
    depth = w_ada.shape[0]
    assert depth == 1
    hg_w = HG_HEADS * HG_DIM
    for l in range(depth):
        mod = _modulation(c, w_ada[l], b_ada[l])
        w_att, w_rest = _prep_w_in(w_in[l])
        z_att = _norm_matmul(x, mod, g_norm1[l], w_att, ATT_WIDTH)
        z_rest = _norm_matmul(x, mod, g_norm1[l], w_rest, 1024)
        y_att = _attention(z_att, positions)
        y_hg = _hgrn(z_rest, hg_lower_bounds, g_head[l], l)
        w_r = jnp.pad(jnp.concatenate([w_router_expert[l], w_router_group[l]], axis=1),
                      ((0, 0), (0, LANES - N_EXPERTS - N_GROUPS)))
        b_r = jnp.pad(jnp.concatenate([b_router_expert[l], b_router_group[l]]),
                      (0, LANES - N_EXPERTS - N_GROUPS)).reshape(1, LANES)
        x1, h3, cw3 = _merge(y_att, y_hg, z_rest, 4 * hg_w // x.shape[-1], x, mod, g_norm2[l],
                             w_attn_up[l].astype(BF16), w_hgrn_up[l].astype(BF16), w_out[l].astype(BF16), w_r, b_r)
        y3 = _moe(h3, cw3, w_exp_gate[l], w_exp_up[l], w_exp_down[l])
        x = _final(x1, y3, mod, g_final)
    return x
```

```python
import functools
import math

import jax
import jax.numpy as jnp
from jax import lax
from jax.experimental import pallas as pl
from jax.experimental.pallas import tpu as pltpu

F32 = jnp.float32
BF16 = jnp.bfloat16
I32 = jnp.int32

ATT_HEADS = 8
ATT_KV_HEADS = 2
ATT_GROUP = ATT_HEADS // ATT_KV_HEADS
HEAD_DIM = 128
IDX_HEADS = 8
IDX_DIM = 64
TOPK_MAX = 256
Q_BLOCK = 128
KEY_CHUNK = 512
ROPE_THETA = 10000.0
ATT_SCALE = HEAD_DIM ** -0.5
IDX_SCALE = (IDX_HEADS * IDX_DIM) ** -0.5
HG_HEADS = 8
HG_DIM = 128
HG_CHUNK = 64
HG_GROUP_ROWS = 512
N_GROUPS = 4
EXPERTS_PER_GROUP = 8
N_EXPERTS = N_GROUPS * EXPERTS_PER_GROUP
EPS = 1e-6

LANES = 128
SUBLANES = 8
INT_MIN = -(2 ** 31)
NEG_INF_KEY = INT_MIN + 0x7FFFFF
MIB = 1024 * 1024

AQ_OFF = 0
AK_OFF = ATT_HEADS * HEAD_DIM
AV_OFF = AK_OFF + ATT_KV_HEADS * HEAD_DIM
IQ_OFF = AV_OFF + ATT_KV_HEADS * HEAD_DIM
IK_OFF = IQ_OFF + IDX_HEADS * IDX_DIM
ATT_COLS = IK_OFF + IDX_DIM + IDX_HEADS
ATT_WIDTH = IK_OFF + LANES


def _sigmoid(x):
    return 1.0 / (1.0 + jnp.exp(-x))


def _silu(x):
    return x * _sigmoid(x)


def _params(sem, vmem_mib):
    return pltpu.CompilerParams(dimension_semantics=sem, vmem_limit_bytes=vmem_mib * MIB)


def _mod_kernel(c_ref, w_ref, b_ref, o_ref):
    c = c_ref[...]
    ca = _silu(c).astype(BF16)
    o_ref[...] = jnp.dot(ca, w_ref[...].astype(BF16), preferred_element_type=F32) + b_ref[...]


def _modulation(c, w, b):
    bsz, d = c.shape
    n = w.shape[1]
    rows = 8
    cp = jnp.zeros((rows, d), F32).at[:bsz].set(c)
    tn = 1024
    out = pl.pallas_call(
        _mod_kernel,
        grid=(n // tn,),
        in_specs=[pl.BlockSpec((rows, d), lambda j: (0, 0)),
                  pl.BlockSpec((d, tn), lambda j: (0, j)),
                  pl.BlockSpec((1, tn), lambda j: (0, j))],
        out_specs=pl.BlockSpec((rows, tn), lambda j: (0, j)),
        out_shape=jax.ShapeDtypeStruct((rows, n), F32),
        compiler_params=_params(("arbitrary",), 40),
        name="modulation",
    )(cp, w, b.reshape(1, n))
    return out[:bsz].reshape(bsz, 6, d)


def _w_head_kernel(w_ref, o_ref, *, n_valid):
    w = w_ref[...]
    lane = lax.broadcasted_iota(I32, w.shape, 1)
    o_ref[...] = jnp.where(lane < n_valid, w, 0.0).astype(BF16)


def _w_shift_kernel(a_ref, b_ref, o_ref, *, shift):
    ab = jnp.concatenate([a_ref[...], b_ref[...]], axis=1)
    o_ref[...] = ab[:, shift:shift + o_ref.shape[1]].astype(BF16)


def _prep_w_in(w, layer):
    _, d, n = w.shape
    tr = min(512, d)
    tn = 1024
    base = (ATT_COLS // LANES) * LANES
    rest = n - ATT_COLS
    assert base % tn == 0 and rest % tn == 0 and ATT_WIDTH <= n
    w_att = pl.pallas_call(
        functools.partial(_w_head_kernel, n_valid=ATT_COLS),
        grid=(d // tr,),
        in_specs=[pl.BlockSpec((None, tr, ATT_WIDTH), lambda r: (layer, r, 0))],
        out_specs=pl.BlockSpec((tr, ATT_WIDTH), lambda r: (r, 0)),
        out_shape=jax.ShapeDtypeStruct((d, ATT_WIDTH), BF16),
        compiler_params=_params(("arbitrary",), 40),
        name="w_in_head",
    )(w)
    w_rest = pl.pallas_call(
        functools.partial(_w_shift_kernel, shift=ATT_COLS - base),
        grid=(d // tr, rest // tn),
        in_specs=[pl.BlockSpec((None, tr, tn), lambda r, j: (layer, r, base // tn + j)),
                  pl.BlockSpec((None, tr, LANES), lambda r, j: (layer, r, (base + (j + 1) * tn) // LANES))],
        out_specs=pl.BlockSpec((tr, tn), lambda r, j: (r, j)),
        out_shape=jax.ShapeDtypeStruct((d, rest), BF16),
        compiler_params=_params(("arbitrary", "arbitrary"), 40),
        name="w_in_rest",
    )(w, w)
    return w_att, w_rest


def _ada_norm(x, g, shift, scale):
    ms = jnp.mean(x * x, axis=-1, keepdims=True)
    y = x * lax.rsqrt(ms + EPS) * g
    return y * (1.0 + scale) + shift


def _norm_matmul_kernel(x_ref, mod_ref, g_ref, w_ref, o_ref, h_ref):
    @pl.when(pl.program_id(2) == 0)
    def _():
        h = _ada_norm(x_ref[...], g_ref[...], mod_ref[0:1, :], mod_ref[1:2, :])
        h_ref[...] = h.astype(BF16)

    o_ref[...] = jnp.dot(h_ref[...], w_ref[...], preferred_element_type=F32)


def _norm_matmul(x, mod, g, w, tn):
    bsz, s, d = x.shape
    n = w.shape[1]
    tm = min(1024 if tn <= 1024 else 512, s)
    return pl.pallas_call(
        _norm_matmul_kernel,
        grid=(bsz, s // tm, n // tn),
        in_specs=[pl.BlockSpec((None, tm, d), lambda b, i, j: (b, i, 0)),
                  pl.BlockSpec((None, 6, d), lambda b, i, j: (b, 0, 0)),
                  pl.BlockSpec((1, d), lambda b, i, j: (0, 0)),
                  pl.BlockSpec((d, tn), lambda b, i, j: (0, j))],
        out_specs=pl.BlockSpec((None, tm, tn), lambda b, i, j: (b, i, j)),
        out_shape=jax.ShapeDtypeStruct((bsz, s, n), F32),
        scratch_shapes=[pltpu.VMEM((tm, d), BF16)],
        compiler_params=_params(("arbitrary", "arbitrary", "arbitrary"), 48),
        name="norm_matmul",
    )(x, mod, g.reshape(1, d), w)


def _sortable(score):
    bits = pltpu.bitcast(score + 0.0, I32)
    return jnp.where(bits < 0, bits ^ 0x7FFFFFFF, bits)


def _attn_kernel(pos_ref, z_ref, o_ref, k_s, vt_s, ik_s, key_s, bias_s, lg_s, acc_s, thr_s, *, topk):
    i = pl.program_id(1)
    qb = Q_BLOCK
    kc = KEY_CHUNK
    n_chunks = (i * qb + kc) // kc
    r0 = pl.multiple_of(i * qb, qb)

    @pl.when(jnp.logical_and(pl.program_id(0) == 0, i == 0))
    def _():
        k_s[...] = jnp.zeros_like(k_s)
        vt_s[...] = jnp.zeros_like(vt_s)
        ik_s[...] = jnp.zeros_like(ik_s)

    lane = lax.broadcasted_iota(I32, (1, LANES), 1)
    pos = pos_ref[...].astype(F32)
    inv_a = jnp.exp((lane % (HEAD_DIM // 2)).astype(F32) * (-2.0 * math.log(ROPE_THETA) / HEAD_DIM))
    ang_a = pos * inv_a
    cos_a = jnp.cos(ang_a)
    sin_a = jnp.where(lane < HEAD_DIM // 2, -1.0, 1.0) * jnp.sin(ang_a)
    inv_i = jnp.exp((lane % (IDX_DIM // 2)).astype(F32) * (-2.0 * math.log(ROPE_THETA) / IDX_DIM))
    ang_i = pos * inv_i
    low_i = (lane % IDX_DIM) < IDX_DIM // 2
    cos_i = jnp.cos(ang_i)
    sin_i = jnp.where(low_i, -1.0, 1.0) * jnp.sin(ang_i)

    def rope_a(x):
        return x * cos_a + pltpu.roll(x, HEAD_DIM // 2, 1) * sin_a

    def rope_i(x):
        partner = jnp.where(low_i, pltpu.roll(x, LANES - IDX_DIM // 2, 1), pltpu.roll(x, IDX_DIM // 2, 1))
        return x * cos_i + partner * sin_i

    for h in range(ATT_KV_HEADS):
        kh = rope_a(z_ref[:, AK_OFF + h * HEAD_DIM:AK_OFF + (h + 1) * HEAD_DIM])
        k_s[pl.ds(r0, qb), h * HEAD_DIM:(h + 1) * HEAD_DIM] = kh.astype(BF16)
    v = z_ref[:, AV_OFF:AV_OFF + ATT_KV_HEADS * HEAD_DIM]
    vt_s[i] = v.T.astype(BF16)
    tile = z_ref[:, IK_OFF:IK_OFF + LANES]
    ikr = rope_i(tile)
    ik2 = jnp.where(lane < IDX_DIM, ikr, pltpu.roll(ikr, IDX_DIM, 1))
    ik_s[pl.ds(r0, qb), :] = ik2.astype(BF16)
    w_t = tile.T[IDX_DIM:IDX_DIM + IDX_HEADS, :] * IDX_SCALE

    q_rows = []
    for t in range(IDX_HEADS // 2):
        iqr = rope_i(z_ref[:, IQ_OFF + t * LANES:IQ_OFF + (t + 1) * LANES])
        q_rows.append(jnp.where(lane < IDX_DIM, iqr, 0.0).astype(BF16))
        q_rows.append(jnp.where(lane < IDX_DIM, 0.0, iqr).astype(BF16))
    iq_all = jnp.concatenate(q_rows, axis=0)

    key_iota = lax.broadcasted_iota(I32, (kc, LANES), 0)
    q_idx = i * qb + lane

    def chunk_rows(j):
        return pl.ds(pl.multiple_of(j * kc, kc), kc)

    zero8 = jnp.zeros((SUBLANES, LANES), I32)

    def count8(mask):
        return jnp.sum(jnp.where(mask, 1, 0).reshape(kc // SUBLANES, SUBLANES, LANES), axis=0)

    def score_body(j, carry):
        dots = lax.dot_general(ik_s[chunk_rows(j), :], iq_all, (((1,), (1,)), ((), ())),
                               preferred_element_type=F32)
        sc = jnp.zeros((kc, LANES), F32)
        for h in range(IDX_HEADS):
            sc = sc + jnp.maximum(dots[:, h * qb:(h + 1) * qb], 0.0) * w_t[h:h + 1, :]
        sc = jnp.where(key_iota + j * kc <= q_idx, sc, -jnp.inf)
        key_s[chunk_rows(j), :] = _sortable(sc)
        return carry

    lax.fori_loop(0, n_chunks, score_body, 0)

    thr_s[...] = jnp.full((1, LANES), NEG_INF_KEY + 1, I32)
    select = i >= topk // qb

    @pl.when(select)
    def _():
        def bit_body(b, t_u):
            cand = t_u | jnp.left_shift(jnp.int32(1), 31 - b)
            cand_s = cand ^ INT_MIN

            def cnt_body(j, acc):
                return acc + count8(key_s[chunk_rows(j), :] >= cand_s)

            cnt = jnp.sum(lax.fori_loop(0, n_chunks, cnt_body, zero8), axis=0, keepdims=True)
            return jnp.where(cnt >= topk, cand, t_u)

        t_u = lax.fori_loop(0, 32, bit_body, jnp.zeros((1, LANES), I32))
        thr_s[...] = t_u ^ INT_MIN

    thr = thr_s[...]

    def bias_body(j, cnt):
        sel = key_s[chunk_rows(j), :] >= thr
        bias_s[chunk_rows(j), :] = jnp.where(sel, 0.0, -jnp.inf)
        return cnt + count8(sel)

    cnt_ge = jnp.sum(lax.fori_loop(0, n_chunks, bias_body, zero8), axis=0, keepdims=True)
    has_ties = jnp.logical_and(select, jnp.max(cnt_ge) > topk)

    @pl.when(has_ties)
    def _():
        def eq_body(j, acc):
            return acc + count8(key_s[chunk_rows(j), :] == thr)

        cnt_eq = jnp.sum(lax.fori_loop(0, n_chunks, eq_body, zero8), axis=0, keepdims=True)
        n_keep = (topk - (cnt_ge - cnt_eq)).astype(F32)
        tril = (lax.broadcasted_iota(I32, (kc, kc), 1) <= lax.broadcasted_iota(I32, (kc, kc), 0)).astype(BF16)

        def tie_body(j, run):
            kk = key_s[chunk_rows(j), :]
            eq = kk == thr
            pref = jnp.dot(tril, jnp.where(eq, 1.0, 0.0).astype(BF16), preferred_element_type=F32) + run
            sel = jnp.logical_or(kk > thr, jnp.logical_and(eq, pref <= n_keep))
            bias_s[chunk_rows(j), :] = jnp.where(sel, 0.0, -jnp.inf)
            return pref[kc - 1:kc, :]

        lax.fori_loop(0, n_chunks, tie_body, jnp.zeros((1, LANES), F32))

    gw = ATT_GROUP * qb
    for h in range(ATT_KV_HEADS):
        qs = jnp.concatenate(
            [(rope_a(z_ref[:, AQ_OFF + (h * ATT_GROUP + g) * HEAD_DIM:AQ_OFF + (h * ATT_GROUP + g + 1) * HEAD_DIM])
              * ATT_SCALE).astype(BF16) for g in range(ATT_GROUP)], axis=0)

        def logit_body(j, m8):
            kch = k_s[chunk_rows(j), h * HEAD_DIM:(h + 1) * HEAD_DIM]
            lt = lax.dot_general(kch, qs, (((1,), (1,)), ((), ())), preferred_element_type=F32)
            bias = bias_s[chunk_rows(j), :]
            lm = lt + jnp.concatenate([bias] * ATT_GROUP, axis=1)
            lg_s[chunk_rows(j), :] = lm
            return jnp.maximum(m8, jnp.max(lm.reshape(kc // 8, 8, gw), axis=0))

        m8 = lax.fori_loop(0, n_chunks, logit_body, jnp.full((8, gw), -jnp.inf, F32))
        m = jnp.max(m8, axis=0, keepdims=True)
        acc_s[...] = jnp.zeros_like(acc_s)

        def pv_body(j, l8):
            p = jnp.exp(lg_s[chunk_rows(j), :] - m)
            vt = jnp.concatenate([vt_s[(kc // qb) * j + t][h * HEAD_DIM:(h + 1) * HEAD_DIM, :]
                                  for t in range(kc // qb)], axis=1)
            acc_s[...] += jnp.dot(vt, p.astype(BF16), preferred_element_type=F32)
            return l8 + jnp.sum(p.reshape(kc // 8, 8, gw), axis=0)

        l8 = lax.fori_loop(0, n_chunks, pv_body, jnp.zeros((8, gw), F32))
        o_t = acc_s[...] / jnp.sum(l8, axis=0, keepdims=True)
        for g in range(ATT_GROUP):
            c0 = (h * ATT_GROUP + g) * HEAD_DIM
            o_ref[:, c0:c0 + HEAD_DIM] = o_t[:, g * qb:(g + 1) * qb].T.astype(o_ref.dtype)


def _attention(z_att, positions):
    bsz, s, _ = z_att.shape
    nblk = s // Q_BLOCK
    topk = min(TOPK_MAX, s // 4)
    assert s % KEY_CHUNK == 0 and KEY_CHUNK % Q_BLOCK == 0 and topk % Q_BLOCK == 0
    return pl.pallas_call(
        functools.partial(_attn_kernel, topk=topk),
        grid=(bsz, nblk),
        in_specs=[pl.BlockSpec((None, Q_BLOCK, 1), lambda b, i: (b, i, 0)),
                  pl.BlockSpec((None, Q_BLOCK, ATT_WIDTH), lambda b, i: (b, i, 0))],
        out_specs=pl.BlockSpec((None, Q_BLOCK, ATT_HEADS * HEAD_DIM), lambda b, i: (b, i, 0)),
        out_shape=jax.ShapeDtypeStruct((bsz, s, ATT_HEADS * HEAD_DIM), BF16),
        scratch_shapes=[pltpu.VMEM((s, ATT_KV_HEADS * HEAD_DIM), BF16),
                        pltpu.VMEM((nblk, ATT_KV_HEADS * HEAD_DIM, Q_BLOCK), BF16),
                        pltpu.VMEM((s, LANES), BF16),
                        pltpu.VMEM((s, LANES), I32),
                        pltpu.VMEM((s, LANES), F32),
                        pltpu.VMEM((s, ATT_GROUP * Q_BLOCK), F32),
                        pltpu.VMEM((HEAD_DIM, ATT_GROUP * Q_BLOCK), F32),
                        pltpu.VMEM((1, LANES), I32)],
        compiler_params=_params(("arbitrary", "arbitrary"), 48),
        name="dsa_attention",
    )(positions.reshape(bsz, s, 1), z_att)


def _hgrn_kernel(lbp_ref, gh_ref, q_ref, f_ref, i_ref, og_ref, o_ref, qd_s, ke_s, intra_s, dec_s, *, layer, seq):
    h = pl.program_id(1)
    c = HG_CHUNK
    gr = HG_GROUP_ROWS
    cpg = gr // c
    lbp = lbp_ref[:, pl.ds(h, 1), :]
    e = jnp.exp(lbp - jnp.max(lbp, axis=0, keepdims=True))
    sm = e / jnp.sum(e, axis=0, keepdims=True)
    lb = sm[0]
    for r in range(1, layer + 1):
        lb = lb + sm[r]
    g_head = gh_ref[pl.ds(h, 1), :]

    row = lax.broadcasted_iota(I32, (gr, HG_DIM), 0)
    in_chunk = row % c
    t_io = lax.broadcasted_iota(I32, (gr, gr), 0)
    s_io = lax.broadcasted_iota(I32, (gr, gr), 1)
    att_mask = jnp.logical_and(t_io // c == s_io // c, s_io <= t_io)

    def group_body(g, carry):
        rows = pl.ds(pl.multiple_of(g * gr, gr), gr)
        q = _silu(q_ref[rows, :])
        f = lb + (1.0 - lb) * _sigmoid(f_ref[rows, :])
        k = 1.0 - f
        cum = jnp.log(f)
        shift = 1
        while shift < c:
            cum = cum + jnp.where(in_chunk >= shift, pltpu.roll(cum, shift, 0), 0.0)
            shift *= 2
        cum3 = cum.reshape(cpg, c, HG_DIM)
        cum_last3 = cum3[:, c - 1:c, :]
        q_dec = (q * jnp.exp(cum)).astype(BF16)
        k_dec = (k * jnp.exp(-cum)).astype(BF16)
        att = lax.dot_general(q_dec, k_dec, (((1,), (1,)), ((), ())), preferred_element_type=F32)
        att = jnp.where(att_mask, att, 0.0)
        intra_s[rows, :] = jnp.dot(att.astype(BF16), i_ref[rows, :].astype(BF16), preferred_element_type=F32)
        qd_s[rows, :] = q_dec
        k_end3 = k.reshape(cpg, c, HG_DIM) * jnp.exp(cum_last3 - cum3)
        ke_s[rows, :] = k_end3.reshape(gr, HG_DIM).astype(BF16)
        dec_s[pl.ds(pl.multiple_of(g * cpg, cpg), cpg), :] = jnp.exp(cum_last3).reshape(cpg, HG_DIM)
        return carry

    lax.fori_loop(0, seq // gr, group_body, 0)

    def chunk_body(n, st):
        rows = pl.ds(pl.multiple_of(n * c, c), c)
        inter = lax.dot_general(qd_s[rows, :], st.astype(BF16), (((1,), (1,)), ((), ())),
                                preferred_element_type=F32)
        o = intra_s[rows, :] + inter
        y = o * lax.rsqrt(jnp.mean(o * o, axis=-1, keepdims=True) + EPS) * g_head
        o_ref[rows, :] = (y * _silu(og_ref[rows, :])).astype(o_ref.dtype)
        upd_t = jnp.dot(i_ref[rows, :].T.astype(BF16), ke_s[rows, :], preferred_element_type=F32)
        return dec_s[pl.ds(n, 1), :] * st + upd_t

    lax.fori_loop(0, seq // c, chunk_body, jnp.zeros((HG_DIM, HG_DIM), F32), unroll=8)


def _hgrn(z_hg, lower_bounds, g_head, layer):
    bsz, s, _ = z_hg.shape
    assert s % HG_GROUP_ROWS == 0 and (HG_GROUP_ROWS // HG_CHUNK) % SUBLANES == 0
    depth1 = lower_bounds.shape[0]
    col = lambda sec: (lambda b, h: (b, 0, sec * HG_HEADS + h))
    specs = [pl.BlockSpec((None, s, HG_DIM), col(sec)) for sec in range(4)]
    return pl.pallas_call(
        functools.partial(_hgrn_kernel, layer=layer, seq=s),
        grid=(bsz, HG_HEADS),
        in_specs=[pl.BlockSpec((depth1, HG_HEADS, HG_DIM), lambda b, h: (0, 0, 0)),
                  pl.BlockSpec((HG_HEADS, HG_DIM), lambda b, h: (0, 0))] + specs,
        out_specs=pl.BlockSpec((None, s, HG_DIM), lambda b, h: (b, 0, h)),
        out_shape=jax.ShapeDtypeStruct((bsz, s, HG_HEADS * HG_DIM), BF16),
        scratch_shapes=[pltpu.VMEM((s, HG_DIM), BF16),
                        pltpu.VMEM((s, HG_DIM), BF16),
                        pltpu.VMEM((s, HG_DIM), F32),
                        pltpu.VMEM((s // HG_CHUNK, HG_DIM), F32)],
        compiler_params=_params(("arbitrary", "arbitrary"), 48),
        name="hgrn2",
    )(lower_bounds.reshape(depth1, HG_HEADS, HG_DIM), g_head, z_hg, z_hg, z_hg, z_hg)


def _merge_kernel(ya_ref, yh_ref, ga_ref, gh_ref, x_ref, mod_ref, g2_ref, wa_ref, wh_ref, wo_ref, wr_ref, br_ref,
                  x1_ref, h3_ref, cw3_ref):
    a = jnp.dot(ya_ref[...], wa_ref[...], preferred_element_type=F32)
    hg = jnp.dot(yh_ref[...], wh_ref[...], preferred_element_type=F32)
    merged = _sigmoid(ga_ref[...]) * a + _sigmoid(gh_ref[...]) * hg
    x1 = x_ref[...] + mod_ref[2:3, :] * jnp.dot(merged.astype(BF16), wo_ref[...], preferred_element_type=F32)
    x1_ref[...] = x1
    h2 = _ada_norm(x1, g2_ref[...], mod_ref[3:4, :], mod_ref[4:5, :])
    for t in range(h2.shape[1] // LANES):
        h3_ref[:, t, :] = h2[:, t * LANES:(t + 1) * LANES]

    wr = wr_ref[...]
    h_hi, w_hi = h2.astype(BF16), wr.astype(BF16)
    h_lo, w_lo = (h2 - h_hi.astype(F32)).astype(BF16), (wr - w_hi.astype(F32)).astype(BF16)
    lg = (jnp.dot(h_hi, w_hi, preferred_element_type=F32)
          + (jnp.dot(h_lo, w_hi, preferred_element_type=F32) + jnp.dot(h_hi, w_lo, preferred_element_type=F32))
          + br_ref[...])
    lane = lax.broadcasted_iota(I32, lg.shape, 1)
    neg = -jnp.inf
    is_grp = jnp.logical_and(lane >= N_EXPERTS, lane < N_EXPERTS + N_GROUPS)
    g_lg = jnp.where(is_grp, lg, neg)
    g_max = jnp.max(g_lg, axis=-1, keepdims=True)
    p_grp = 1.0 / jnp.sum(jnp.exp(g_lg - g_max), axis=-1, keepdims=True)
    grp = jnp.min(jnp.where(g_lg == g_max, lane, 2 * LANES), axis=-1, keepdims=True) - N_EXPERTS
    lo = grp * EXPERTS_PER_GROUP
    in_grp = jnp.logical_and(lane >= lo, lane < lo + EXPERTS_PER_GROUP)
    e_lg = jnp.where(in_grp, lg, neg)
    e_max = jnp.max(e_lg, axis=-1, keepdims=True)
    idx1 = jnp.min(jnp.where(e_lg == e_max, lane, 2 * LANES), axis=-1, keepdims=True)
    e_lg2 = jnp.where(lane == idx1, neg, e_lg)
    e_max2 = jnp.max(e_lg2, axis=-1, keepdims=True)
    idx2 = jnp.min(jnp.where(e_lg2 == e_max2, lane, 2 * LANES), axis=-1, keepdims=True)
    e2 = jnp.exp(e_max2 - e_max)
    p1 = p_grp / (1.0 + e2)
    p2 = p_grp * e2 / (1.0 + e2)
    cw = jnp.where(lane == idx1, p1, jnp.where(lane == idx2, p2, jnp.where(lane == grp + N_EXPERTS, 1.0, 0.0)))
    cw3_ref[...] = jnp.zeros_like(cw3_ref)
    cw3_ref[:, 0, :] = cw


def _merge(y_att, y_hg, z_gate, gate_blk, x, mod, g2, wa, wh, wo, wr, br):
    bsz, s, d = x.shape
    tm = min(256, s)
    dh = y_att.shape[-1]
    const = lambda shape: pl.BlockSpec(shape, lambda b, i: (0,) * len(shape))
    row = lambda width, j=0: pl.BlockSpec((None, tm, width), lambda b, i: (b, i, j))
    slab = lambda rows: pl.BlockSpec((tm, rows, LANES), lambda b, i: (b * (s // tm) + i, 0, 0))
    return pl.pallas_call(
        _merge_kernel,
        grid=(bsz, s // tm),
        in_specs=[row(dh), row(dh), row(d, gate_blk), row(d, gate_blk + 1), row(d),
                  pl.BlockSpec((None, 6, d), lambda b, i: (b, 0, 0)), const((1, d)),
                  const((dh, d)), const((dh, d)), const((d, d)), const((d, LANES)), const((1, LANES))],
        out_specs=[row(d), slab(d // LANES), slab(SUBLANES)],
        out_shape=[jax.ShapeDtypeStruct((bsz, s, d), F32),
                   jax.ShapeDtypeStruct((bsz * s, d // LANES, LANES), F32),
                   jax.ShapeDtypeStruct((bsz * s, SUBLANES, LANES), F32)],
        compiler_params=_params(("arbitrary", "arbitrary"), 56),
        name="merge_router",
    )(y_att, y_hg, z_gate, z_gate, x, mod, g2.reshape(1, d), wa, wh, wo, wr, br)


def _moe_kernel(tg_ref, nv_ref, src_ref, h3_hbm, cw3_hbm, wg_ref, wu_ref, wd_ref, y3_hbm,
                hbuf, cbuf, xb, cwv, acc, ybuf, gsem, ssem, *, tm):
    k = pl.program_id(0)
    e = pl.program_id(1)
    nt = pl.num_programs(0)
    ne = pl.num_programs(1)
    slot = k % 2
    nv = nv_ref[k]
    n_slab = xb.shape[1] // LANES

    def gather_copies(tile, slot_, r):
        tok = src_ref[tile * tm + r]
        return (pltpu.make_async_copy(h3_hbm.at[tok], hbuf.at[slot_, r], gsem.at[slot_]),
                pltpu.make_async_copy(cw3_hbm.at[tok], cbuf.at[slot_, r], gsem.at[slot_]))

    def gather(tile, slot_, start):
        def body(r, carry):
            for cp in gather_copies(tile, slot_, r):
                cp.start() if start else cp.wait()
            return carry

        lax.fori_loop(0, tm, body, 0, unroll=8)

    def scatter(tile, start):
        def body(r, carry):
            cp = pltpu.make_async_copy(ybuf.at[r], y3_hbm.at[src_ref[tile * tm + r]], ssem.at[0])
            cp.start() if start else cp.wait()
            return carry

        lax.fori_loop(0, nv_ref[tile], body, 0)

    @pl.when(e == 0)
    def _():
        @pl.when(jnp.logical_and(k == 0, nv > 0))
        def _():
            gather(k, slot, True)

        nxt = jnp.minimum(k + 1, nt - 1)

        @pl.when(jnp.logical_and(k + 1 < nt, nv_ref[nxt] > 0))
        def _():
            gather(nxt, 1 - slot, True)

        @pl.when(nv > 0)
        def _():
            gather(k, slot, False)
            for t in range(n_slab):
                xb[:, t * LANES:(t + 1) * LANES] = hbuf[slot, :, t, :].astype(BF16)
            cwv[...] = cbuf[slot, :, 0, :]
            acc[...] = jnp.zeros_like(acc)

    @pl.when(nv > 0)
    def _():
        h = xb[...]
        a = jnp.dot(h, wg_ref[...].astype(BF16), preferred_element_type=F32)
        u = jnp.dot(h, wu_ref[...].astype(BF16), preferred_element_type=F32)
        cw = cwv[...]
        lane = lax.broadcasted_iota(I32, cw.shape, 1)
        cwe = jnp.sum(jnp.where(lane == tg_ref[k] * ne + e, cw, 0.0), axis=-1, keepdims=True)
        hm = (_silu(a) * u * cwe).astype(BF16)
        acc[...] += jnp.dot(hm, wd_ref[...].astype(BF16), preferred_element_type=F32)

    @pl.when(e == ne - 1)
    def _():
        @pl.when(k > 0)
        def _():
            scatter(jnp.maximum(k - 1, 0), False)

        @pl.when(nv > 0)
        def _():
            for t in range(n_slab):
                ybuf[:, t, :] = acc[:, t * LANES:(t + 1) * LANES]
            scatter(k, True)

        @pl.when(k == nt - 1)
        def _():
            scatter(k, False)


def _moe_tiles(grp, tm):
    t = grp.shape[0]
    nt = t // tm + N_GROUPS
    order = jnp.argsort(grp, stable=True).astype(I32)
    counts = jnp.sum((grp[:, None] == jnp.arange(N_GROUPS, dtype=I32)[None, :]).astype(I32), axis=0)
    tiles_g = (counts + tm - 1) // tm
    tile_end = jnp.cumsum(tiles_g)
    tile0 = tile_end - tiles_g
    gstart = jnp.cumsum(counts) - counts
    k = jnp.arange(nt, dtype=I32)
    used = k < tile_end[-1]
    tg = jnp.sum((k[:, None] >= tile_end[None, :]).astype(I32), axis=1)
    last_g = jnp.sum((tile_end[-1] - 1 >= tile_end).astype(I32))
    tg = jnp.where(used, tg, last_g)
    q0 = (k - tile0[tg]) * tm
    nvalid = jnp.where(used, jnp.clip(counts[tg] - q0, 0, tm), 0).astype(I32)
    r = jnp.arange(tm, dtype=I32)
    idx = jnp.clip(gstart[tg][:, None] + q0[:, None] + r[None, :], 0, t - 1)
    src = jnp.where(r[None, :] < nvalid[:, None], order[idx], 0).astype(I32)
    return tg.astype(I32), nvalid, src.reshape(nt * tm)


def _moe(h3, cw3, wg, wu, wd):
    t, n_slab, _ = h3.shape
    d = n_slab * LANES
    ne_all, _, ff = wg.shape
    tm = min(512, t)
    grp = jnp.argmax(cw3[:, 0, N_EXPERTS:N_EXPERTS + N_GROUPS], axis=-1).astype(I32)
    tg, nvalid, src = _moe_tiles(grp, tm)
    nt = tg.shape[0]
    w_idx = lambda k, e, tg_r, nv_r, src_r: (tg_r[k] * EXPERTS_PER_GROUP + jnp.where(nv_r[k] > 0, e, EXPERTS_PER_GROUP - 1),
                                           0, 0)
    grid_spec = pltpu.PrefetchScalarGridSpec(
        num_scalar_prefetch=3,
        grid=(nt, EXPERTS_PER_GROUP),
        in_specs=[pl.BlockSpec(memory_space=pl.ANY), pl.BlockSpec(memory_space=pl.ANY),
                  pl.BlockSpec((None, d, ff), w_idx), pl.BlockSpec((None, d, ff), w_idx),
                  pl.BlockSpec((None, ff, d), w_idx)],
        out_specs=pl.BlockSpec(memory_space=pl.ANY),
        scratch_shapes=[pltpu.VMEM((2, tm, n_slab, LANES), F32),
                        pltpu.VMEM((2, tm, SUBLANES, LANES), F32),
                        pltpu.VMEM((tm, d), BF16),
                        pltpu.VMEM((tm, LANES), F32),
                        pltpu.VMEM((tm, d), F32),
                        pltpu.VMEM((tm, n_slab, LANES), F32),
                        pltpu.SemaphoreType.DMA((2,)),
                        pltpu.SemaphoreType.DMA((1,))])
    return pl.pallas_call(
        functools.partial(_moe_kernel, tm=tm),
        grid_spec=grid_spec,
        out_shape=jax.ShapeDtypeStruct((t, n_slab, LANES), F32),
        compiler_params=_params(("arbitrary", "arbitrary"), 56),
        name="moe_experts",
    )(tg, nvalid, src, h3, cw3, wg, wu, wd)


def _final_kernel(x1_ref, y3_ref, mod_ref, g_ref, o_ref, *, gate_row):
    y = jnp.concatenate([y3_ref[:, t, :] for t in range(y3_ref.shape[1])], axis=1)
    x2 = x1_ref[...] + mod_ref[gate_row:gate_row + 1, :] * y
    ms = jnp.mean(x2 * x2, axis=-1, keepdims=True)
    o_ref[...] = x2 * lax.rsqrt(ms + EPS) * g_ref[...]


def _final(x1, y3, mod, g):
    bsz, s, d = x1.shape
    tm = min(512, s)
    return pl.pallas_call(
        functools.partial(_final_kernel, gate_row=5),
        grid=(bsz, s // tm),
        in_specs=[pl.BlockSpec((None, tm, d), lambda b, i: (b, i, 0)),
                  pl.BlockSpec((tm, d // LANES, LANES), lambda b, i: (b * (s // tm) + i, 0, 0)),
                  pl.BlockSpec((None, 6, d), lambda b, i: (b, 0, 0)),
                  pl.BlockSpec((1, d), lambda b, i: (0, 0))],
        out_specs=pl.BlockSpec((None, tm, d), lambda b, i: (b, i, 0)),
        out_shape=jax.ShapeDtypeStruct((bsz, s, d), F32),
        compiler_params=_params(("arbitrary", "arbitrary"), 40),
        name="residual_final_norm",
    )(x1, y3, mod, g.reshape(1, d))


def kernel(x, c, positions, w_ada, b_ada, g_norm1, w_in, g_head, hg_lower_bounds, w_attn_up, w_hgrn_up, w_out,
           g_norm2, w_router_group, b_router_group, w_router_expert, b_router_expert, w_exp_gate, w_exp_up,
           w_exp_down, g_final):
    depth = w_ada.shape[0]
    assert depth == 1
    hg_w = HG_HEADS * HG_DIM
    for l in range(depth):
        mod = _modulation(c, w_ada[l], b_ada[l])
        w_att, w_rest = _prep_w_in(w_in, l)
        z_att = _norm_matmul(x, mod, g_norm1[l], w_att, ATT_WIDTH)
        z_rest = _norm_matmul(x, mod, g_norm1[l], w_rest, 1024)
        y_att = _attention(z_att, positions)
        y_hg = _hgrn(z_rest, hg_lower_bounds, g_head[l], l)
        w_r = jnp.pad(jnp.concatenate([w_router_expert[l], w_router_group[l]], axis=1),
                      ((0, 0), (0, LANES - N_EXPERTS - N_GROUPS)))
        b_r = jnp.pad(jnp.concatenate([b_router_expert[l], b_router_group[l]]),
                      (0, LANES - N_EXPERTS - N_GROUPS)).reshape(1, LANES)
        x1, h3, cw3 = _merge(y_att, y_hg, z_rest, 4 * hg_w // x.shape[-1], x, mod, g_norm2[l],
                             w_attn_up[l].astype(BF16), w_hgrn_up[l].astype(BF16), w_out[l].astype(BF16), w_r, b_r)
        y3 = _moe(h3, cw3, w_exp_gate[l], w_exp_up[l], w_exp_down[l])
        x = _final(x1, y3, mod, g_final)
    return x
```

```python
import functools
import math

import jax
import jax.numpy as jnp
from jax import lax
from jax.experimental import pallas as pl
from jax.experimental.pallas import tpu as pltpu

F32 = jnp.float32
BF16 = jnp.bfloat16
I32 = jnp.int32

ATT_HEADS = 8
ATT_KV_HEADS = 2
ATT_GROUP = ATT_HEADS // ATT_KV_HEADS
HEAD_DIM = 128
IDX_HEADS = 8
IDX_DIM = 64
TOPK_MAX = 256
Q_BLOCK = 128
KEY_CHUNK = 512
ROPE_THETA = 10000.0
ATT_SCALE = HEAD_DIM ** -0.5
IDX_SCALE = (IDX_HEADS * IDX_DIM) ** -0.5
HG_HEADS = 8
HG_DIM = 128
HG_CHUNK = 64
HG_GROUP_ROWS = 512
N_GROUPS = 4
EXPERTS_PER_GROUP = 8
N_EXPERTS = N_GROUPS * EXPERTS_PER_GROUP
EPS = 1e-6

LANES = 128
SUBLANES = 8
INT_MIN = -(2 ** 31)
NEG_INF_KEY = INT_MIN + 0x7FFFFF
MIB = 1024 * 1024

AQ_OFF = 0
AK_OFF = ATT_HEADS * HEAD_DIM
AV_OFF = AK_OFF + ATT_KV_HEADS * HEAD_DIM
IQ_OFF = AV_OFF + ATT_KV_HEADS * HEAD_DIM
IK_OFF = IQ_OFF + IDX_HEADS * IDX_DIM
ATT_COLS = IK_OFF + IDX_DIM + IDX_HEADS
ATT_WIDTH = IK_OFF + LANES


def _sigmoid(x):
    return 1.0 / (1.0 + jnp.exp(-x))


def _silu(x):
    return x * _sigmoid(x)


def _params(sem, vmem_mib):
    return pltpu.CompilerParams(dimension_semantics=sem, vmem_limit_bytes=vmem_mib * MIB)


def _mod_kernel(c_ref, w_ref, b_ref, o_ref):
    c = c_ref[...]
    ca = _silu(c).astype(BF16)
    o_ref[...] = jnp.dot(ca, w_ref[...].astype(BF16), preferred_element_type=F32) + b_ref[...]


def _modulation(c, w, b):
    bsz, d = c.shape
    n = w.shape[1]
    rows = 8
    cp = jnp.zeros((rows, d), F32).at[:bsz].set(c)
    tn = 1024
    out = pl.pallas_call(
        _mod_kernel,
        grid=(n // tn,),
        in_specs=[pl.BlockSpec((rows, d), lambda j: (0, 0)),
                  pl.BlockSpec((d, tn), lambda j: (0, j)),
                  pl.BlockSpec((1, tn), lambda j: (0, j))],
        out_specs=pl.BlockSpec((rows, tn), lambda j: (0, j)),
        out_shape=jax.ShapeDtypeStruct((rows, n), F32),
        compiler_params=_params(("arbitrary",), 40),
        name="modulation",
    )(cp, w, b.reshape(1, n))
    return out[:bsz].reshape(bsz, 6, d)


def _w_head_kernel(w_ref, o_ref, *, n_valid):
    w = w_ref[...]
    lane = lax.broadcasted_iota(I32, w.shape, 1)
    o_ref[...] = jnp.where(lane < n_valid, w, 0.0).astype(BF16)


def _w_shift_kernel(a_ref, b_ref, o_ref, *, shift):
    ab = jnp.concatenate([a_ref[...], b_ref[...]], axis=1)
    o_ref[...] = ab[:, shift:shift + o_ref.shape[1]].astype(BF16)


def _prep_w_in(w, layer):
    _, d, n = w.shape
    tr = min(512, d)
    tn = 1024
    base = (ATT_COLS // LANES) * LANES
    rest = n - ATT_COLS
    assert base % tn == 0 and rest % tn == 0 and ATT_WIDTH <= n
    w_att = pl.pallas_call(
        functools.partial(_w_head_kernel, n_valid=ATT_COLS),
        grid=(d // tr,),
        in_specs=[pl.BlockSpec((None, tr, ATT_WIDTH), lambda r: (layer, r, 0))],
        out_specs=pl.BlockSpec((tr, ATT_WIDTH), lambda r: (r, 0)),
        out_shape=jax.ShapeDtypeStruct((d, ATT_WIDTH), BF16),
        compiler_params=_params(("arbitrary",), 40),
        name="w_in_head",
    )(w)
    w_rest = pl.pallas_call(
        functools.partial(_w_shift_kernel, shift=ATT_COLS - base),
        grid=(d // tr, rest // tn),
        in_specs=[pl.BlockSpec((None, tr, tn), lambda r, j: (layer, r, base // tn + j)),
                  pl.BlockSpec((None, tr, LANES), lambda r, j: (layer, r, (base + (j + 1) * tn) // LANES))],
        out_specs=pl.BlockSpec((tr, tn), lambda r, j: (r, j)),
        out_shape=jax.ShapeDtypeStruct((d, rest), BF16),
        compiler_params=_params(("arbitrary", "arbitrary"), 40),
        name="w_in_rest",
    )(w, w)
    return w_att, w_rest


def _ada_norm(x, g, shift, scale):
    ms = jnp.mean(x * x, axis=-1, keepdims=True)
    y = x * lax.rsqrt(ms + EPS) * g
    return y * (1.0 + scale) + shift


def _norm_matmul_kernel(x_ref, mod_ref, g_ref, w_ref, o_ref, h_ref):
    @pl.when(pl.program_id(2) == 0)
    def _():
        h = _ada_norm(x_ref[...], g_ref[...], mod_ref[0:1, :], mod_ref[1:2, :])
        h_ref[...] = h.astype(BF16)

    o_ref[...] = jnp.dot(h_ref[...], w_ref[...], preferred_element_type=F32)


def _norm_matmul(x, mod, g, w, tn):
    bsz, s, d = x.shape
    n = w.shape[1]
    tm = min(1024 if tn <= 1024 else 512, s)
    return pl.pallas_call(
        _norm_matmul_kernel,
        grid=(bsz, s // tm, n // tn),
        in_specs=[pl.BlockSpec((None, tm, d), lambda b, i, j: (b, i, 0)),
                  pl.BlockSpec((None, 6, d), lambda b, i, j: (b, 0, 0)),
                  pl.BlockSpec((1, d), lambda b, i, j: (0, 0)),
                  pl.BlockSpec((d, tn), lambda b, i, j: (0, j))],
        out_specs=pl.BlockSpec((None, tm, tn), lambda b, i, j: (b, i, j)),
        out_shape=jax.ShapeDtypeStruct((bsz, s, n), F32),
        scratch_shapes=[pltpu.VMEM((tm, d), BF16)],
        compiler_params=_params(("arbitrary", "arbitrary", "arbitrary"), 48),
        name="norm_matmul",
    )(x, mod, g.reshape(1, d), w)


def _sortable(score):
    bits = pltpu.bitcast(score + 0.0, I32)
    return jnp.where(bits < 0, bits ^ 0x7FFFFFFF, bits)


def _attn_kernel(pos_ref, z_ref, o_ref, k_s, vt_s, ik_s, key_s, bias_s, lg_s, acc_s, thr_s, *, topk):
    i = pl.program_id(1)
    qb = Q_BLOCK
    kc = KEY_CHUNK
    n_chunks = (i * qb + kc) // kc
    r0 = pl.multiple_of(i * qb, qb)

    @pl.when(jnp.logical_and(pl.program_id(0) == 0, i == 0))
    def _():
        k_s[...] = jnp.zeros_like(k_s)
        vt_s[...] = jnp.zeros_like(vt_s)
        ik_s[...] = jnp.zeros_like(ik_s)

    lane = lax.broadcasted_iota(I32, (1, LANES), 1)
    pos = pos_ref[...].astype(F32)
    inv_a = jnp.exp((lane % (HEAD_DIM // 2)).astype(F32) * (-2.0 * math.log(ROPE_THETA) / HEAD_DIM))
    ang_a = pos * inv_a
    cos_a = jnp.cos(ang_a)
    sin_a = jnp.where(lane < HEAD_DIM // 2, -1.0, 1.0) * jnp.sin(ang_a)
    inv_i = jnp.exp((lane % (IDX_DIM // 2)).astype(F32) * (-2.0 * math.log(ROPE_THETA) / IDX_DIM))
    ang_i = pos * inv_i
    low_i = (lane % IDX_DIM) < IDX_DIM // 2
    cos_i = jnp.cos(ang_i)
    sin_i = jnp.where(low_i, -1.0, 1.0) * jnp.sin(ang_i)

    def rope_a(x):
        return x * cos_a + pltpu.roll(x, HEAD_DIM // 2, 1) * sin_a

    def rope_i(x):
        partner = jnp.where(low_i, pltpu.roll(x, LANES - IDX_DIM // 2, 1), pltpu.roll(x, IDX_DIM // 2, 1))
        return x * cos_i + partner * sin_i

    for h in range(ATT_KV_HEADS):
        kh = rope_a(z_ref[:, AK_OFF + h * HEAD_DIM:AK_OFF + (h + 1) * HEAD_DIM])
        k_s[pl.ds(r0, qb), h * HEAD_DIM:(h + 1) * HEAD_DIM] = kh.astype(BF16)
    v = z_ref[:, AV_OFF:AV_OFF + ATT_KV_HEADS * HEAD_DIM]
    vt_s[i] = v.T.astype(BF16)
    tile = z_ref[:, IK_OFF:IK_OFF + LANES]
    ikr = rope_i(tile)
    ik2 = jnp.where(lane < IDX_DIM, ikr, pltpu.roll(ikr, IDX_DIM, 1))
    ik_s[pl.ds(r0, qb), :] = ik2.astype(BF16)
    w_t = tile.T[IDX_DIM:IDX_DIM + IDX_HEADS, :] * IDX_SCALE

    q_rows = []
    for t in range(IDX_HEADS // 2):
        iqr = rope_i(z_ref[:, IQ_OFF + t * LANES:IQ_OFF + (t + 1) * LANES])
        q_rows.append(jnp.where(lane < IDX_DIM, iqr, 0.0).astype(BF16))
        q_rows.append(jnp.where(lane < IDX_DIM, 0.0, iqr).astype(BF16))
    iq_all = jnp.concatenate(q_rows, axis=0)

    key_iota = lax.broadcasted_iota(I32, (kc, LANES), 0)
    q_idx = i * qb + lane

    def chunk_rows(j):
        return pl.ds(pl.multiple_of(j * kc, kc), kc)

    zero8 = jnp.zeros((SUBLANES, LANES), I32)

    def count8(mask):
        return jnp.sum(jnp.where(mask, 1, 0).reshape(kc // SUBLANES, SUBLANES, LANES), axis=0)

    def score_body(j, carry):
        dots = lax.dot_general(ik_s[chunk_rows(j), :], iq_all, (((1,), (1,)), ((), ())),
                               preferred_element_type=F32)
        sc = jnp.zeros((kc, LANES), F32)
        for h in range(IDX_HEADS):
            sc = sc + jnp.maximum(dots[:, h * qb:(h + 1) * qb], 0.0) * w_t[h:h + 1, :]
        sc = jnp.where(key_iota + j * kc <= q_idx, sc, -jnp.inf)
        key_s[chunk_rows(j), :] = _sortable(sc)
        return carry

    lax.fori_loop(0, n_chunks, score_body, 0)

    thr_s[...] = jnp.full((1, LANES), NEG_INF_KEY + 1, I32)
    select = i >= topk // qb

    @pl.when(select)
    def _():
        def bit_body(b, t_u):
            cand = t_u | jnp.left_shift(jnp.int32(1), 31 - b)
            cand_s = cand ^ INT_MIN

            def cnt_body(j, acc):
                return acc + count8(key_s[chunk_rows(j), :] >= cand_s)

            cnt = jnp.sum(lax.fori_loop(0, n_chunks, cnt_body, zero8), axis=0, keepdims=True)
            return jnp.where(cnt >= topk, cand, t_u)

        t_u = lax.fori_loop(0, 32, bit_body, jnp.zeros((1, LANES), I32))
        thr_s[...] = t_u ^ INT_MIN

    thr = thr_s[...]

    def bias_body(j, cnt):
        sel = key_s[chunk_rows(j), :] >= thr
        bias_s[chunk_rows(j), :] = jnp.where(sel, 0.0, -jnp.inf)
        return cnt + count8(sel)

    cnt_ge = jnp.sum(lax.fori_loop(0, n_chunks, bias_body, zero8), axis=0, keepdims=True)
    has_ties = jnp.logical_and(select, jnp.max(cnt_ge) > topk)

    @pl.when(has_ties)
    def _():
        def eq_body(j, acc):
            return acc + count8(key_s[chunk_rows(j), :] == thr)

        cnt_eq = jnp.sum(lax.fori_loop(0, n_chunks, eq_body, zero8), axis=0, keepdims=True)
        n_keep = (topk - (cnt_ge - cnt_eq)).astype(F32)
        tril = (lax.broadcasted_iota(I32, (kc, kc), 1) <= lax.broadcasted_iota(I32, (kc, kc), 0)).astype(BF16)

        def tie_body(j, run):
            kk = key_s[chunk_rows(j), :]
            eq = kk == thr
            pref = jnp.dot(tril, jnp.where(eq, 1.0, 0.0).astype(BF16), preferred_element_type=F32) + run
            sel = jnp.logical_or(kk > thr, jnp.logical_and(eq, pref <= n_keep))
            bias_s[chunk_rows(j), :] = jnp.where(sel, 0.0, -jnp.inf)
            return pref[kc - 1:kc, :]

        lax.fori_loop(0, n_chunks, tie_body, jnp.zeros((1, LANES), F32))

    gw = ATT_GROUP * qb
    for h in range(ATT_KV_HEADS):
        qs = jnp.concatenate(
            [(rope_a(z_ref[:, AQ_OFF + (h * ATT_GROUP + g) * HEAD_DIM:AQ_OFF + (h * ATT_GROUP + g + 1) * HEAD_DIM])
              * ATT_SCALE).astype(BF16) for g in range(ATT_GROUP)], axis=0)

        def logit_body(j, m8):
            kch = k_s[chunk_rows(j), h * HEAD_DIM:(h + 1) * HEAD_DIM]
            lt = lax.dot_general(kch, qs, (((1,), (1,)), ((), ())), preferred_element_type=F32)
            bias = bias_s[chunk_rows(j), :]
            lm = lt + jnp.concatenate([bias] * ATT_GROUP, axis=1)
            lg_s[chunk_rows(j), :] = lm
            return jnp.maximum(m8, jnp.max(lm.reshape(kc // 8, 8, gw), axis=0))

        m8 = lax.fori_loop(0, n_chunks, logit_body, jnp.full((8, gw), -jnp.inf, F32))
        m = jnp.max(m8, axis=0, keepdims=True)
        acc_s[...] = jnp.zeros_like(acc_s)

        def pv_body(j, l8):
            p = jnp.exp(lg_s[chunk_rows(j), :] - m)
            vt = jnp.concatenate([vt_s[(kc // qb) * j + t][h * HEAD_DIM:(h + 1) * HEAD_DIM, :]
                                  for t in range(kc // qb)], axis=1)
            acc_s[...] += jnp.dot(vt, p.astype(BF16), preferred_element_type=F32)
            return l8 + jnp.sum(p.reshape(kc // 8, 8, gw), axis=0)

        l8 = lax.fori_loop(0, n_chunks, pv_body, jnp.zeros((8, gw), F32))
        o_t = acc_s[...] / jnp.sum(l8, axis=0, keepdims=True)
        for g in range(ATT_GROUP):
            c0 = (h * ATT_GROUP + g) * HEAD_DIM
            o_ref[:, c0:c0 + HEAD_DIM] = o_t[:, g * qb:(g + 1) * qb].T.astype(o_ref.dtype)


def _attention(z_att, positions):
    bsz, s, _ = z_att.shape
    nblk = s // Q_BLOCK
    topk = min(TOPK_MAX, s // 4)
    assert s % KEY_CHUNK == 0 and KEY_CHUNK % Q_BLOCK == 0 and topk % Q_BLOCK == 0
    return pl.pallas_call(
        functools.partial(_attn_kernel, topk=topk),
        grid=(bsz, nblk),
        in_specs=[pl.BlockSpec((None, Q_BLOCK, 1), lambda b, i: (b, i, 0)),
                  pl.BlockSpec((None, Q_BLOCK, ATT_WIDTH), lambda b, i: (b, i, 0))],
        out_specs=pl.BlockSpec((None, Q_BLOCK, ATT_HEADS * HEAD_DIM), lambda b, i: (b, i, 0)),
        out_shape=jax.ShapeDtypeStruct((bsz, s, ATT_HEADS * HEAD_DIM), BF16),
        scratch_shapes=[pltpu.VMEM((s, ATT_KV_HEADS * HEAD_DIM), BF16),
                        pltpu.VMEM((nblk, ATT_KV_HEADS * HEAD_DIM, Q_BLOCK), BF16),
                        pltpu.VMEM((s, LANES), BF16),
                        pltpu.VMEM((s, LANES), I32),
                        pltpu.VMEM((s, LANES), F32),
                        pltpu.VMEM((s, ATT_GROUP * Q_BLOCK), F32),
                        pltpu.VMEM((HEAD_DIM, ATT_GROUP * Q_BLOCK), F32),
                        pltpu.VMEM((1, LANES), I32)],
        compiler_params=_params(("arbitrary", "arbitrary"), 48),
        name="dsa_attention",
    )(positions.reshape(bsz, s, 1), z_att)


def _hgrn_kernel(lbp_ref, gh_ref, q_ref, f_ref, i_ref, og_ref, o_ref, qd_s, ke_s, intra_s, dec_s, *, layer, seq):
    h = pl.program_id(1)
    c = HG_CHUNK
    gr = HG_GROUP_ROWS
    cpg = gr // c
    lbp = lbp_ref[:, pl.ds(h, 1), :]
    e = jnp.exp(lbp - jnp.max(lbp, axis=0, keepdims=True))
    sm = e / jnp.sum(e, axis=0, keepdims=True)
    lb = sm[0]
    for r in range(1, layer + 1):
        lb = lb + sm[r]
    g_head = gh_ref[pl.ds(h, 1), :]

    row = lax.broadcasted_iota(I32, (gr, HG_DIM), 0)
    in_chunk = row % c
    t_io = lax.broadcasted_iota(I32, (gr, gr), 0)
    s_io = lax.broadcasted_iota(I32, (gr, gr), 1)
    att_mask = jnp.logical_and(t_io // c == s_io // c, s_io <= t_io)

    def group_body(g, carry):
        rows = pl.ds(pl.multiple_of(g * gr, gr), gr)
        q = _silu(q_ref[rows, :])
        f = lb + (1.0 - lb) * _sigmoid(f_ref[rows, :])
        k = 1.0 - f
        cum = jnp.log(f)
        shift = 1
        while shift < c:
            cum = cum + jnp.where(in_chunk >= shift, pltpu.roll(cum, shift, 0), 0.0)
            shift *= 2
        cum3 = cum.reshape(cpg, c, HG_DIM)
        cum_last3 = cum3[:, c - 1:c, :]
        q_dec = (q * jnp.exp(cum)).astype(BF16)
        k_dec = (k * jnp.exp(-cum)).astype(BF16)
        att = lax.dot_general(q_dec, k_dec, (((1,), (1,)), ((), ())), preferred_element_type=F32)
        att = jnp.where(att_mask, att, 0.0)
        intra_s[rows, :] = jnp.dot(att.astype(BF16), i_ref[rows, :].astype(BF16), preferred_element_type=F32)
        qd_s[rows, :] = q_dec
        k_end3 = k.reshape(cpg, c, HG_DIM) * jnp.exp(cum_last3 - cum3)
        ke_s[rows, :] = k_end3.reshape(gr, HG_DIM).astype(BF16)
        dec_s[pl.ds(pl.multiple_of(g * cpg, cpg), cpg), :] = jnp.exp(cum_last3).reshape(cpg, HG_DIM)
        return carry

    lax.fori_loop(0, seq // gr, group_body, 0)

    def chunk_body(n, st):
        rows = pl.ds(pl.multiple_of(n * c, c), c)
        inter = lax.dot_general(qd_s[rows, :], st.astype(BF16), (((1,), (1,)), ((), ())),
                                preferred_element_type=F32)
        o = intra_s[rows, :] + inter
        y = o * lax.rsqrt(jnp.mean(o * o, axis=-1, keepdims=True) + EPS) * g_head
        o_ref[rows, :] = (y * _silu(og_ref[rows, :])).astype(o_ref.dtype)
        upd_t = jnp.dot(i_ref[rows, :].T.astype(BF16), ke_s[rows, :], preferred_element_type=F32)
        return dec_s[pl.ds(n, 1), :] * st + upd_t

    lax.fori_loop(0, seq // c, chunk_body, jnp.zeros((HG_DIM, HG_DIM), F32), unroll=8)


def _hgrn(z_hg, lower_bounds, g_head, layer):
    bsz, s, _ = z_hg.shape
    assert s % HG_GROUP_ROWS == 0 and (HG_GROUP_ROWS // HG_CHUNK) % SUBLANES == 0
    depth1 = lower_bounds.shape[0]
    col = lambda sec: (lambda b, h: (b, 0, sec * HG_HEADS + h))
    specs = [pl.BlockSpec((None, s, HG_DIM), col(sec)) for sec in range(4)]
    return pl.pallas_call(
        functools.partial(_hgrn_kernel, layer=layer, seq=s),
        grid=(bsz, HG_HEADS),
        in_specs=[pl.BlockSpec((depth1, HG_HEADS, HG_DIM), lambda b, h: (0, 0, 0)),
                  pl.BlockSpec((HG_HEADS, HG_DIM), lambda b, h: (0, 0))] + specs,
        out_specs=pl.BlockSpec((None, s, HG_DIM), lambda b, h: (b, 0, h)),
        out_shape=jax.ShapeDtypeStruct((bsz, s, HG_HEADS * HG_DIM), BF16),
        scratch_shapes=[pltpu.VMEM((s, HG_DIM), BF16),
                        pltpu.VMEM((s, HG_DIM), BF16),
                        pltpu.VMEM((s, HG_DIM), F32),
                        pltpu.VMEM((s // HG_CHUNK, HG_DIM), F32)],
        compiler_params=_params(("arbitrary", "arbitrary"), 48),
        name="hgrn2",
    )(lower_bounds.reshape(depth1, HG_HEADS, HG_DIM), g_head, z_hg, z_hg, z_hg, z_hg)


def _merge_kernel(ya_ref, yh_ref, ga_ref, gh_ref, x_ref, mod_ref, g2_ref, wa_ref, wh_ref, wo_ref, wr_ref, br_ref,
                  x1_ref, h3_ref, cw3_ref, grp_ref):
    a = jnp.dot(ya_ref[...], wa_ref[...], preferred_element_type=F32)
    hg = jnp.dot(yh_ref[...], wh_ref[...], preferred_element_type=F32)
    merged = _sigmoid(ga_ref[...]) * a + _sigmoid(gh_ref[...]) * hg
    x1 = x_ref[...] + mod_ref[2:3, :] * jnp.dot(merged.astype(BF16), wo_ref[...], preferred_element_type=F32)
    x1_ref[...] = x1
    h2 = _ada_norm(x1, g2_ref[...], mod_ref[3:4, :], mod_ref[4:5, :])
    for t in range(h2.shape[1] // LANES):
        h3_ref[:, t, :] = h2[:, t * LANES:(t + 1) * LANES]

    wr = wr_ref[...]
    h_hi, w_hi = h2.astype(BF16), wr.astype(BF16)
    h_lo, w_lo = (h2 - h_hi.astype(F32)).astype(BF16), (wr - w_hi.astype(F32)).astype(BF16)
    lg = (jnp.dot(h_hi, w_hi, preferred_element_type=F32)
          + (jnp.dot(h_lo, w_hi, preferred_element_type=F32) + jnp.dot(h_hi, w_lo, preferred_element_type=F32))
          + br_ref[...])
    lane = lax.broadcasted_iota(I32, lg.shape, 1)
    neg = -jnp.inf
    is_grp = jnp.logical_and(lane >= N_EXPERTS, lane < N_EXPERTS + N_GROUPS)
    g_lg = jnp.where(is_grp, lg, neg)
    g_max = jnp.max(g_lg, axis=-1, keepdims=True)
    p_grp = 1.0 / jnp.sum(jnp.exp(g_lg - g_max), axis=-1, keepdims=True)
    grp = jnp.min(jnp.where(g_lg == g_max, lane, 2 * LANES), axis=-1, keepdims=True) - N_EXPERTS
    lo = grp * EXPERTS_PER_GROUP
    in_grp = jnp.logical_and(lane >= lo, lane < lo + EXPERTS_PER_GROUP)
    e_lg = jnp.where(in_grp, lg, neg)
    e_max = jnp.max(e_lg, axis=-1, keepdims=True)
    idx1 = jnp.min(jnp.where(e_lg == e_max, lane, 2 * LANES), axis=-1, keepdims=True)
    e_lg2 = jnp.where(lane == idx1, neg, e_lg)
    e_max2 = jnp.max(e_lg2, axis=-1, keepdims=True)
    idx2 = jnp.min(jnp.where(e_lg2 == e_max2, lane, 2 * LANES), axis=-1, keepdims=True)
    e2 = jnp.exp(e_max2 - e_max)
    p1 = p_grp / (1.0 + e2)
    p2 = p_grp * e2 / (1.0 + e2)
    cw = jnp.where(lane == idx1, p1, jnp.where(lane == idx2, p2, 0.0))
    grp_ref[...] = grp
    cw3_ref[...] = jnp.zeros_like(cw3_ref)
    cw3_ref[:, 0, :] = cw


def _merge(y_att, y_hg, z_gate, gate_blk, x, mod, g2, wa, wh, wo, wr, br):
    bsz, s, d = x.shape
    tm = min(256, s)
    dh = y_att.shape[-1]
    const = lambda shape: pl.BlockSpec(shape, lambda b, i: (0,) * len(shape))
    row = lambda width, j=0: pl.BlockSpec((None, tm, width), lambda b, i: (b, i, j))
    slab = lambda rows: pl.BlockSpec((tm, rows, LANES), lambda b, i: (b * (s // tm) + i, 0, 0))
    return pl.pallas_call(
        _merge_kernel,
        grid=(bsz, s // tm),
        in_specs=[row(dh), row(dh), row(d, gate_blk), row(d, gate_blk + 1), row(d),
                  pl.BlockSpec((None, 6, d), lambda b, i: (b, 0, 0)), const((1, d)),
                  const((dh, d)), const((dh, d)), const((d, d)), const((d, LANES)), const((1, LANES))],
        out_specs=[row(d), slab(d // LANES), slab(SUBLANES),
                   pl.BlockSpec((tm, 1), lambda b, i: (b * (s // tm) + i, 0))],
        out_shape=[jax.ShapeDtypeStruct((bsz, s, d), F32),
                   jax.ShapeDtypeStruct((bsz * s, d // LANES, LANES), F32),
                   jax.ShapeDtypeStruct((bsz * s, SUBLANES, LANES), F32),
                   jax.ShapeDtypeStruct((bsz * s, 1), I32)],
        compiler_params=_params(("arbitrary", "arbitrary"), 56),
        name="merge_router",
    )(y_att, y_hg, z_gate, z_gate, x, mod, g2.reshape(1, d), wa, wh, wo, wr, br)


def _moe_kernel(tg_ref, nv_ref, src_ref, h3_hbm, cw3_hbm, wg_ref, wu_ref, wd_ref, y3_hbm,
                hbuf, cbuf, xb, cwv, acc, ybuf, gsem, ssem, *, tm):
    k = pl.program_id(0)
    e = pl.program_id(1)
    nt = pl.num_programs(0)
    ne = pl.num_programs(1)
    slot = k % 2
    nv = nv_ref[k]
    n_slab = xb.shape[1] // LANES

    def gather_copies(tile, slot_, r):
        tok = src_ref[tile * tm + r]
        return (pltpu.make_async_copy(h3_hbm.at[tok], hbuf.at[slot_, r], gsem.at[slot_]),
                pltpu.make_async_copy(cw3_hbm.at[tok], cbuf.at[slot_, r], gsem.at[slot_]))

    def gather(tile, slot_, start):
        def body(r, carry):
            for cp in gather_copies(tile, slot_, r):
                cp.start() if start else cp.wait()
            return carry

        lax.fori_loop(0, tm, body, 0, unroll=8)

    def scatter(tile, start):
        def body(r, carry):
            cp = pltpu.make_async_copy(ybuf.at[r], y3_hbm.at[src_ref[tile * tm + r]], ssem.at[0])
            cp.start() if start else cp.wait()
            return carry

        lax.fori_loop(0, nv_ref[tile], body, 0)

    @pl.when(e == 0)
    def _():
        @pl.when(jnp.logical_and(k == 0, nv > 0))
        def _():
            gather(k, slot, True)

        nxt = jnp.minimum(k + 1, nt - 1)

        @pl.when(jnp.logical_and(k + 1 < nt, nv_ref[nxt] > 0))
        def _():
            gather(nxt, 1 - slot, True)

        @pl.when(nv > 0)
        def _():
            gather(k, slot, False)
            for t in range(n_slab):
                xb[:, t * LANES:(t + 1) * LANES] = hbuf[slot, :, t, :].astype(BF16)
            cwv[...] = cbuf[slot, :, 0, :]
            acc[...] = jnp.zeros_like(acc)

    @pl.when(nv > 0)
    def _():
        h = xb[...]
        a = jnp.dot(h, wg_ref[...].astype(BF16), preferred_element_type=F32)
        u = jnp.dot(h, wu_ref[...].astype(BF16), preferred_element_type=F32)
        cw = cwv[...]
        lane = lax.broadcasted_iota(I32, cw.shape, 1)
        cwe = jnp.sum(jnp.where(lane == tg_ref[k] * ne + e, cw, 0.0), axis=-1, keepdims=True)
        hm = (_silu(a) * u * cwe).astype(BF16)
        acc[...] += jnp.dot(hm, wd_ref[...].astype(BF16), preferred_element_type=F32)

    @pl.when(e == ne - 1)
    def _():
        @pl.when(k > 0)
        def _():
            scatter(jnp.maximum(k - 1, 0), False)

        @pl.when(nv > 0)
        def _():
            for t in range(n_slab):
                ybuf[:, t, :] = acc[:, t * LANES:(t + 1) * LANES]
            scatter(k, True)

        @pl.when(k == nt - 1)
        def _():
            scatter(k, False)


def _moe_tiles(grp, tm):
    t = grp.shape[0]
    nt = t // tm + N_GROUPS
    order = jnp.argsort(grp, stable=True).astype(I32)
    counts = jnp.sum((grp[:, None] == jnp.arange(N_GROUPS, dtype=I32)[None, :]).astype(I32), axis=0)
    tiles_g = (counts + tm - 1) // tm
    tile_end = jnp.cumsum(tiles_g)
    tile0 = tile_end - tiles_g
    gstart = jnp.cumsum(counts) - counts
    k = jnp.arange(nt, dtype=I32)
    used = k < tile_end[-1]
    tg = jnp.sum((k[:, None] >= tile_end[None, :]).astype(I32), axis=1)
    last_g = jnp.sum((tile_end[-1] - 1 >= tile_end).astype(I32))
    tg = jnp.where(used, tg, last_g)
    q0 = (k - tile0[tg]) * tm
    nvalid = jnp.where(used, jnp.clip(counts[tg] - q0, 0, tm), 0).astype(I32)
    r = jnp.arange(tm, dtype=I32)
    idx = jnp.clip(gstart[tg][:, None] + q0[:, None] + r[None, :], 0, t - 1)
    src = jnp.where(r[None, :] < nvalid[:, None], order[idx], 0).astype(I32)
    return tg.astype(I32), nvalid, src.reshape(nt * tm)


def _moe(h3, cw3, grp, wg, wu, wd):
    t, n_slab, _ = h3.shape
    d = n_slab * LANES
    ne_all, _, ff = wg.shape
    tm = min(512, t)
    tg, nvalid, src = _moe_tiles(grp.reshape(t), tm)
    nt = tg.shape[0]
    w_idx = lambda k, e, tg_r, nv_r, src_r: (tg_r[k] * EXPERTS_PER_GROUP + jnp.where(nv_r[k] > 0, e, EXPERTS_PER_GROUP - 1),
                                           0, 0)
    grid_spec = pltpu.PrefetchScalarGridSpec(
        num_scalar_prefetch=3,
        grid=(nt, EXPERTS_PER_GROUP),
        in_specs=[pl.BlockSpec(memory_space=pl.ANY), pl.BlockSpec(memory_space=pl.ANY),
                  pl.BlockSpec((None, d, ff), w_idx), pl.BlockSpec((None, d, ff), w_idx),
                  pl.BlockSpec((None, ff, d), w_idx)],
        out_specs=pl.BlockSpec(memory_space=pl.ANY),
        scratch_shapes=[pltpu.VMEM((2, tm, n_slab, LANES), F32),
                        pltpu.VMEM((2, tm, SUBLANES, LANES), F32),
                        pltpu.VMEM((tm, d), BF16),
                        pltpu.VMEM((tm, LANES), F32),
                        pltpu.VMEM((tm, d), F32),
                        pltpu.VMEM((tm, n_slab, LANES), F32),
                        pltpu.SemaphoreType.DMA((2,)),
                        pltpu.SemaphoreType.DMA((1,))])
    return pl.pallas_call(
        functools.partial(_moe_kernel, tm=tm),
        grid_spec=grid_spec,
        out_shape=jax.ShapeDtypeStruct((t, n_slab, LANES), F32),
        compiler_params=_params(("arbitrary", "arbitrary"), 56),
        name="moe_experts",
    )(tg, nvalid, src, h3, cw3, wg, wu, wd)


def _final_kernel(x1_ref, y3_ref, mod_ref, g_ref, o_ref, *, gate_row):
    y = jnp.concatenate([y3_ref[:, t, :] for t in range(y3_ref.shape[1])], axis=1)
    x2 = x1_ref[...] + mod_ref[gate_row:gate_row + 1, :] * y
    ms = jnp.mean(x2 * x2, axis=-1, keepdims=True)
    o_ref[...] = x2 * lax.rsqrt(ms + EPS) * g_ref[...]


def _final(x1, y3, mod, g):
    bsz, s, d = x1.shape
    tm = min(512, s)
    return pl.pallas_call(
        functools.partial(_final_kernel, gate_row=5),
        grid=(bsz, s // tm),
        in_specs=[pl.BlockSpec((None, tm, d), lambda b, i: (b, i, 0)),
                  pl.BlockSpec((tm, d // LANES, LANES), lambda b, i: (b * (s // tm) + i, 0, 0)),
                  pl.BlockSpec((None, 6, d), lambda b, i: (b, 0, 0)),
                  pl.BlockSpec((1, d), lambda b, i: (0, 0))],
        out_specs=pl.BlockSpec((None, tm, d), lambda b, i: (b, i, 0)),
        out_shape=jax.ShapeDtypeStruct((bsz, s, d), F32),
        compiler_params=_params(("arbitrary", "arbitrary"), 40),
        name="residual_final_norm",
    )(x1, y3, mod, g.reshape(1, d))


def kernel(x, c, positions, w_ada, b_ada, g_norm1, w_in, g_head, hg_lower_bounds, w_attn_up, w_hgrn_up, w_out,
           g_norm2, w_router_group, b_router_group, w_router_expert, b_router_expert, w_exp_gate, w_exp_up,
           w_exp_down, g_final):
    depth = w_ada.shape[0]
    assert depth == 1
    hg_w = HG_HEADS * HG_DIM
    for l in range(depth):
        mod = _modulation(c, w_ada[l], b_ada[l])
        w_att, w_rest = _prep_w_in(w_in, l)
        z_att = _norm_matmul(x, mod, g_norm1[l], w_att, ATT_WIDTH)
        z_rest = _norm_matmul(x, mod, g_norm1[l], w_rest, 1024)
        y_att = _attention(z_att, positions)
        y_hg = _hgrn(z_rest, hg_lower_bounds, g_head[l], l)
        w_r = jnp.pad(jnp.concatenate([w_router_expert[l], w_router_group[l]], axis=1),
                      ((0, 0), (0, LANES - N_EXPERTS - N_GROUPS)))
        b_r = jnp.pad(jnp.concatenate([b_router_expert[l], b_router_group[l]]),
                      (0, LANES - N_EXPERTS - N_GROUPS)).reshape(1, LANES)
        x1, h3, cw3, grp = _merge(y_att, y_hg, z_rest, 4 * hg_w // x.shape[-1], x, mod, g_norm2[l],
                             w_attn_up[l].astype(BF16), w_hgrn_up[l].astype(BF16), w_out[l].astype(BF16), w_r, b_r)
        y3 = _moe(h3, cw3, grp, w_exp_gate[l], w_exp_up[l], w_exp_down[l])
        x = _final(x1, y3, mod, g_final)
    return x
```

```python
import functools
import math

import jax
import jax.numpy as jnp
from jax import lax
from jax.experimental import pallas as pl
from jax.experimental.pallas import tpu as pltpu

F32 = jnp.float32
BF16 = jnp.bfloat16
I32 = jnp.int32

ATT_HEADS = 8
ATT_KV_HEADS = 2
ATT_GROUP = ATT_HEADS // ATT_KV_HEADS
HEAD_DIM = 128
IDX_HEADS = 8
IDX_DIM = 64
TOPK_MAX = 256
Q_BLOCK = 128
KEY_CHUNK = 512
ROPE_THETA = 10000.0
ATT_SCALE = HEAD_DIM ** -0.5
IDX_SCALE = (IDX_HEADS * IDX_DIM) ** -0.5
HG_HEADS = 8
HG_DIM = 128
HG_CHUNK = 64
HG_GROUP_ROWS = 512
N_GROUPS = 4
EXPERTS_PER_GROUP = 8
N_EXPERTS = N_GROUPS * EXPERTS_PER_GROUP
EPS = 1e-6

LANES = 128
SUBLANES = 8
INT_MIN = -(2 ** 31)
NEG_INF_KEY = INT_MIN + 0x7FFFFF
MIB = 1024 * 1024

AQ_OFF = 0
AK_OFF = ATT_HEADS * HEAD_DIM
AV_OFF = AK_OFF + ATT_KV_HEADS * HEAD_DIM
IQ_OFF = AV_OFF + ATT_KV_HEADS * HEAD_DIM
IK_OFF = IQ_OFF + IDX_HEADS * IDX_DIM
ATT_COLS = IK_OFF + IDX_DIM + IDX_HEADS
ATT_WIDTH = IK_OFF + LANES


def _sigmoid(x):
    return 1.0 / (1.0 + jnp.exp(-x))


def _silu(x):
    return x * _sigmoid(x)


def _params(sem, vmem_mib):
    return pltpu.CompilerParams(dimension_semantics=sem, vmem_limit_bytes=vmem_mib * MIB)


def _mod_kernel(c_ref, w_ref, b_ref, o_ref):
    c = c_ref[...]
    ca = _silu(c).astype(BF16)
    o_ref[...] = jnp.dot(ca, w_ref[...].astype(BF16), preferred_element_type=F32) + b_ref[...]


def _modulation(c, w, b):
    bsz, d = c.shape
    n = w.shape[1]
    rows = 8
    cp = jnp.zeros((rows, d), F32).at[:bsz].set(c)
    tn = 1024
    out = pl.pallas_call(
        _mod_kernel,
        grid=(n // tn,),
        in_specs=[pl.BlockSpec((rows, d), lambda j: (0, 0)),
                  pl.BlockSpec((d, tn), lambda j: (0, j)),
                  pl.BlockSpec((1, tn), lambda j: (0, j))],
        out_specs=pl.BlockSpec((rows, tn), lambda j: (0, j)),
        out_shape=jax.ShapeDtypeStruct((rows, n), F32),
        compiler_params=_params(("arbitrary",), 40),
        name="modulation",
    )(cp, w, b.reshape(1, n))
    return out[:bsz].reshape(bsz, 6, d)


def _w_head_kernel(w_ref, o_ref, *, n_valid):
    w = w_ref[...]
    lane = lax.broadcasted_iota(I32, w.shape, 1)
    o_ref[...] = jnp.where(lane < n_valid, w, 0.0).astype(BF16)


def _w_shift_kernel(a_ref, b_ref, o_ref, *, shift):
    ab = jnp.concatenate([a_ref[...], b_ref[...]], axis=1)
    o_ref[...] = ab[:, shift:shift + o_ref.shape[1]].astype(BF16)


def _prep_w_in(w, layer):
    _, d, n = w.shape
    tr = min(512, d)
    tn = 1024
    base = (ATT_COLS // LANES) * LANES
    rest = n - ATT_COLS
    assert base % tn == 0 and rest % tn == 0 and ATT_WIDTH <= n
    w_att = pl.pallas_call(
        functools.partial(_w_head_kernel, n_valid=ATT_COLS),
        grid=(d // tr,),
        in_specs=[pl.BlockSpec((None, tr, ATT_WIDTH), lambda r: (layer, r, 0))],
        out_specs=pl.BlockSpec((tr, ATT_WIDTH), lambda r: (r, 0)),
        out_shape=jax.ShapeDtypeStruct((d, ATT_WIDTH), BF16),
        compiler_params=_params(("arbitrary",), 40),
        name="w_in_head",
    )(w)
    w_rest = pl.pallas_call(
        functools.partial(_w_shift_kernel, shift=ATT_COLS - base),
        grid=(d // tr, rest // tn),
        in_specs=[pl.BlockSpec((None, tr, tn), lambda r, j: (layer, r, base // tn + j)),
                  pl.BlockSpec((None, tr, LANES), lambda r, j: (layer, r, (base + (j + 1) * tn) // LANES))],
        out_specs=pl.BlockSpec((tr, tn), lambda r, j: (r, j)),
        out_shape=jax.ShapeDtypeStruct((d, rest), BF16),
        compiler_params=_params(("arbitrary", "arbitrary"), 40),
        name="w_in_rest",
    )(w, w)
    return w_att, w_rest


def _ada_norm(x, g, shift, scale):
    ms = jnp.mean(x * x, axis=-1, keepdims=True)
    y = x * lax.rsqrt(ms + EPS) * g
    return y * (1.0 + scale) + shift


def _norm_matmul_kernel(x_ref, mod_ref, g_ref, w_ref, o_ref, h_ref):
    @pl.when(pl.program_id(2) == 0)
    def _():
        h = _ada_norm(x_ref[...], g_ref[...], mod_ref[0:1, :], mod_ref[1:2, :])
        h_ref[...] = h.astype(BF16)

    o_ref[...] = jnp.dot(h_ref[...], w_ref[...], preferred_element_type=F32)


def _norm_matmul(x, mod, g, w, tn):
    bsz, s, d = x.shape
    n = w.shape[1]
    tm = min(1024 if tn <= 1024 else 512, s)
    return pl.pallas_call(
        _norm_matmul_kernel,
        grid=(bsz, s // tm, n // tn),
        in_specs=[pl.BlockSpec((None, tm, d), lambda b, i, j: (b, i, 0)),
                  pl.BlockSpec((None, 6, d), lambda b, i, j: (b, 0, 0)),
                  pl.BlockSpec((1, d), lambda b, i, j: (0, 0)),
                  pl.BlockSpec((d, tn), lambda b, i, j: (0, j))],
        out_specs=pl.BlockSpec((None, tm, tn), lambda b, i, j: (b, i, j)),
        out_shape=jax.ShapeDtypeStruct((bsz, s, n), F32),
        scratch_shapes=[pltpu.VMEM((tm, d), BF16)],
        compiler_params=_params(("arbitrary", "arbitrary", "arbitrary"), 48),
        name="norm_matmul",
    )(x, mod, g.reshape(1, d), w)


def _sortable(score):
    bits = pltpu.bitcast(score + 0.0, I32)
    return jnp.where(bits < 0, bits ^ 0x7FFFFFFF, bits)


def _rope_kernel(pos_ref, ca_ref, sa_ref, ci_ref, si_ref):
    lane = lax.broadcasted_iota(I32, (1, LANES), 1)
    pos = pos_ref[...].astype(F32)
    inv_a = jnp.exp((lane % (HEAD_DIM // 2)).astype(F32) * (-2.0 * math.log(ROPE_THETA) / HEAD_DIM))
    ang_a = pos * inv_a
    ca_ref[...] = jnp.cos(ang_a)
    sa_ref[...] = jnp.where(lane < HEAD_DIM // 2, -1.0, 1.0) * jnp.sin(ang_a)
    inv_i = jnp.exp((lane % (IDX_DIM // 2)).astype(F32) * (-2.0 * math.log(ROPE_THETA) / IDX_DIM))
    ang_i = pos * inv_i
    ci_ref[...] = jnp.cos(ang_i)
    si_ref[...] = jnp.where((lane % IDX_DIM) < IDX_DIM // 2, -1.0, 1.0) * jnp.sin(ang_i)


def _rope_tables(positions):
    bsz, s = positions.shape
    tr = min(512, s)
    tab = pl.BlockSpec((None, tr, LANES), lambda b, i: (b, i, 0))
    return pl.pallas_call(
        _rope_kernel,
        grid=(bsz, s // tr),
        in_specs=[pl.BlockSpec((None, tr, 1), lambda b, i: (b, i, 0))],
        out_specs=[tab] * 4,
        out_shape=[jax.ShapeDtypeStruct((bsz, s, LANES), F32)] * 4,
        compiler_params=_params(("arbitrary", "arbitrary"), 32),
        name="rope_tables",
    )(positions.reshape(bsz, s, 1))


def _attn_kernel(ca_ref, sa_ref, ci_ref, si_ref, z_ref, o_ref, k_s, vt_s, ik_s, key_s, bias_s, lg_s, acc_s, thr_s,
                 *, topk):
    i = pl.program_id(1)
    qb = Q_BLOCK
    kc = KEY_CHUNK
    n_chunks = (i * qb + kc) // kc
    r0 = pl.multiple_of(i * qb, qb)

    @pl.when(jnp.logical_and(pl.program_id(0) == 0, i == 0))
    def _():
        k_s[...] = jnp.zeros_like(k_s)
        vt_s[...] = jnp.zeros_like(vt_s)
        ik_s[...] = jnp.zeros_like(ik_s)

    lane = lax.broadcasted_iota(I32, (1, LANES), 1)
    cos_a, sin_a, cos_i, sin_i = ca_ref[...], sa_ref[...], ci_ref[...], si_ref[...]
    low_i = (lane % IDX_DIM) < IDX_DIM // 2

    def rope_a(x):
        return x * cos_a + pltpu.roll(x, HEAD_DIM // 2, 1) * sin_a

    def rope_i(x):
        partner = jnp.where(low_i, pltpu.roll(x, LANES - IDX_DIM // 2, 1), pltpu.roll(x, IDX_DIM // 2, 1))
        return x * cos_i + partner * sin_i

    for h in range(ATT_KV_HEADS):
        kh = rope_a(z_ref[:, AK_OFF + h * HEAD_DIM:AK_OFF + (h + 1) * HEAD_DIM])
        k_s[pl.ds(r0, qb), h * HEAD_DIM:(h + 1) * HEAD_DIM] = kh.astype(BF16)
    v = z_ref[:, AV_OFF:AV_OFF + ATT_KV_HEADS * HEAD_DIM]
    vt_s[i] = v.T.astype(BF16)
    tile = z_ref[:, IK_OFF:IK_OFF + LANES]
    ikr = rope_i(tile)
    ik2 = jnp.where(lane < IDX_DIM, ikr, pltpu.roll(ikr, IDX_DIM, 1))
    ik_s[pl.ds(r0, qb), :] = ik2.astype(BF16)
    w_t = tile.T[IDX_DIM:IDX_DIM + IDX_HEADS, :] * IDX_SCALE

    q_rows = []
    for t in range(IDX_HEADS // 2):
        iqr = rope_i(z_ref[:, IQ_OFF + t * LANES:IQ_OFF + (t + 1) * LANES])
        q_rows.append(jnp.where(lane < IDX_DIM, iqr, 0.0).astype(BF16))
        q_rows.append(jnp.where(lane < IDX_DIM, 0.0, iqr).astype(BF16))
    iq_all = jnp.concatenate(q_rows, axis=0)

    key_iota = lax.broadcasted_iota(I32, (kc, LANES), 0)
    q_idx = i * qb + lane

    def chunk_rows(j):
        return pl.ds(pl.multiple_of(j * kc, kc), kc)

    zero8 = jnp.zeros((SUBLANES, LANES), I32)

    def count8(mask):
        return jnp.sum(jnp.where(mask, 1, 0).reshape(kc // SUBLANES, SUBLANES, LANES), axis=0)

    def score_body(j, carry):
        dots = lax.dot_general(ik_s[chunk_rows(j), :], iq_all, (((1,), (1,)), ((), ())),
                               preferred_element_type=F32)
        sc = jnp.zeros((kc, LANES), F32)
        for h in range(IDX_HEADS):
            sc = sc + jnp.maximum(dots[:, h * qb:(h + 1) * qb], 0.0) * w_t[h:h + 1, :]
        sc = jnp.where(key_iota + j * kc <= q_idx, sc, -jnp.inf)
        key_s[chunk_rows(j), :] = _sortable(sc)
        return carry

    lax.fori_loop(0, n_chunks, score_body, 0)

    thr_s[...] = jnp.full((1, LANES), NEG_INF_KEY + 1, I32)
    select = i >= topk // qb

    @pl.when(select)
    def _():
        def bit_body(b, t_u):
            cand = t_u | jnp.left_shift(jnp.int32(1), 31 - b)
            cand_s = cand ^ INT_MIN

            def cnt_body(j, acc):
                return acc + count8(key_s[chunk_rows(j), :] >= cand_s)

            cnt = jnp.sum(lax.fori_loop(0, n_chunks, cnt_body, zero8), axis=0, keepdims=True)
            return jnp.where(cnt >= topk, cand, t_u)

        t_u = lax.fori_loop(0, 32, bit_body, jnp.zeros((1, LANES), I32))
        thr_s[...] = t_u ^ INT_MIN

    thr = thr_s[...]

    def bias_body(j, cnt):
        sel = key_s[chunk_rows(j), :] >= thr
        bias_s[chunk_rows(j), :] = jnp.where(sel, 0.0, -jnp.inf)
        return cnt + count8(sel)

    cnt_ge = jnp.sum(lax.fori_loop(0, n_chunks, bias_body, zero8), axis=0, keepdims=True)
    has_ties = jnp.logical_and(select, jnp.max(cnt_ge) > topk)

    @pl.when(has_ties)
    def _():
        def eq_body(j, acc):
            return acc + count8(key_s[chunk_rows(j), :] == thr)

        cnt_eq = jnp.sum(lax.fori_loop(0, n_chunks, eq_body, zero8), axis=0, keepdims=True)
        n_keep = (topk - (cnt_ge - cnt_eq)).astype(F32)
        tril = (lax.broadcasted_iota(I32, (kc, kc), 1) <= lax.broadcasted_iota(I32, (kc, kc), 0)).astype(BF16)

        def tie_body(j, run):
            kk = key_s[chunk_rows(j), :]
            eq = kk == thr
            pref = jnp.dot(tril, jnp.where(eq, 1.0, 0.0).astype(BF16), preferred_element_type=F32) + run
            sel = jnp.logical_or(kk > thr, jnp.logical_and(eq, pref <= n_keep))
            bias_s[chunk_rows(j), :] = jnp.where(sel, 0.0, -jnp.inf)
            return pref[kc - 1:kc, :]

        lax.fori_loop(0, n_chunks, tie_body, jnp.zeros((1, LANES), F32))

    gw = ATT_GROUP * qb
    for h in range(ATT_KV_HEADS):
        qs = jnp.concatenate(
            [(rope_a(z_ref[:, AQ_OFF + (h * ATT_GROUP + g) * HEAD_DIM:AQ_OFF + (h * ATT_GROUP + g + 1) * HEAD_DIM])
              * ATT_SCALE).astype(BF16) for g in range(ATT_GROUP)], axis=0)

        def logit_body(j, m8):
            kch = k_s[chunk_rows(j), h * HEAD_DIM:(h + 1) * HEAD_DIM]
            lt = lax.dot_general(kch, qs, (((1,), (1,)), ((), ())), preferred_element_type=F32)
            bias = bias_s[chunk_rows(j), :]
            lm = lt + jnp.concatenate([bias] * ATT_GROUP, axis=1)
            lg_s[chunk_rows(j), :] = lm
            return jnp.maximum(m8, jnp.max(lm.reshape(kc // 8, 8, gw), axis=0))

        m8 = lax.fori_loop(0, n_chunks, logit_body, jnp.full((8, gw), -jnp.inf, F32))
        m = jnp.max(m8, axis=0, keepdims=True)
        acc_s[...] = jnp.zeros_like(acc_s)

        def pv_body(j, l8):
            p = jnp.exp(lg_s[chunk_rows(j), :] - m)
            vt = jnp.concatenate([vt_s[(kc // qb) * j + t][h * HEAD_DIM:(h + 1) * HEAD_DIM, :]
                                  for t in range(kc // qb)], axis=1)
            acc_s[...] += jnp.dot(vt, p.astype(BF16), preferred_element_type=F32)
            return l8 + jnp.sum(p.reshape(kc // 8, 8, gw), axis=0)

        l8 = lax.fori_loop(0, n_chunks, pv_body, jnp.zeros((8, gw), F32))
        o_t = acc_s[...] / jnp.sum(l8, axis=0, keepdims=True)
        for g in range(ATT_GROUP):
            c0 = (h * ATT_GROUP + g) * HEAD_DIM
            o_ref[:, c0:c0 + HEAD_DIM] = o_t[:, g * qb:(g + 1) * qb].T.astype(o_ref.dtype)


def _attention(z_att, positions):
    bsz, s, _ = z_att.shape
    nblk = s // Q_BLOCK
    topk = min(TOPK_MAX, s // 4)
    assert s % KEY_CHUNK == 0 and KEY_CHUNK % Q_BLOCK == 0 and topk % Q_BLOCK == 0
    return pl.pallas_call(
        functools.partial(_attn_kernel, topk=topk),
        grid=(bsz, nblk),
        in_specs=[pl.BlockSpec((None, Q_BLOCK, LANES), lambda b, i: (b, i, 0))] * 4
                 + [pl.BlockSpec((None, Q_BLOCK, ATT_WIDTH), lambda b, i: (b, i, 0))],
        out_specs=pl.BlockSpec((None, Q_BLOCK, ATT_HEADS * HEAD_DIM), lambda b, i: (b, i, 0)),
        out_shape=jax.ShapeDtypeStruct((bsz, s, ATT_HEADS * HEAD_DIM), BF16),
        scratch_shapes=[pltpu.VMEM((s, ATT_KV_HEADS * HEAD_DIM), BF16),
                        pltpu.VMEM((nblk, ATT_KV_HEADS * HEAD_DIM, Q_BLOCK), BF16),
                        pltpu.VMEM((s, LANES), BF16),
                        pltpu.VMEM((s, LANES), I32),
                        pltpu.VMEM((s, LANES), F32),
                        pltpu.VMEM((s, ATT_GROUP * Q_BLOCK), F32),
                        pltpu.VMEM((HEAD_DIM, ATT_GROUP * Q_BLOCK), F32),
                        pltpu.VMEM((1, LANES), I32)],
        compiler_params=_params(("arbitrary", "arbitrary"), 48),
        name="dsa_attention",
    )(*_rope_tables(positions), z_att)


def _hgrn_kernel(lbp_ref, gh_ref, q_ref, f_ref, i_ref, og_ref, o_ref, qd_s, ke_s, intra_s, dec_s, *, layer, seq):
    h = pl.program_id(1)
    c = HG_CHUNK
    gr = HG_GROUP_ROWS
    cpg = gr // c
    lbp = lbp_ref[:, pl.ds(h, 1), :]
    e = jnp.exp(lbp - jnp.max(lbp, axis=0, keepdims=True))
    sm = e / jnp.sum(e, axis=0, keepdims=True)
    lb = sm[0]
    for r in range(1, layer + 1):
        lb = lb + sm[r]
    g_head = gh_ref[pl.ds(h, 1), :]

    row = lax.broadcasted_iota(I32, (gr, HG_DIM), 0)
    in_chunk = row % c
    t_io = lax.broadcasted_iota(I32, (gr, gr), 0)
    s_io = lax.broadcasted_iota(I32, (gr, gr), 1)
    att_mask = jnp.logical_and(t_io // c == s_io // c, s_io <= t_io)

    def group_body(g, carry):
        rows = pl.ds(pl.multiple_of(g * gr, gr), gr)
        q = _silu(q_ref[rows, :])
        f = lb + (1.0 - lb) * _sigmoid(f_ref[rows, :])
        k = 1.0 - f
        cum = jnp.log(f)
        shift = 1
        while shift < c:
            cum = cum + jnp.where(in_chunk >= shift, pltpu.roll(cum, shift, 0), 0.0)
            shift *= 2
        cum3 = cum.reshape(cpg, c, HG_DIM)
        cum_last3 = cum3[:, c - 1:c, :]
        q_dec = (q * jnp.exp(cum)).astype(BF16)
        k_dec = (k * jnp.exp(-cum)).astype(BF16)
        att = lax.dot_general(q_dec, k_dec, (((1,), (1,)), ((), ())), preferred_element_type=F32)
        att = jnp.where(att_mask, att, 0.0)
        intra_s[rows, :] = jnp.dot(att.astype(BF16), i_ref[rows, :].astype(BF16), preferred_element_type=F32)
        qd_s[rows, :] = q_dec
        k_end3 = k.reshape(cpg, c, HG_DIM) * jnp.exp(cum_last3 - cum3)
        ke_s[rows, :] = k_end3.reshape(gr, HG_DIM).astype(BF16)
        dec_s[pl.ds(pl.multiple_of(g * cpg, cpg), cpg), :] = jnp.exp(cum_last3).reshape(cpg, HG_DIM)
        return carry

    lax.fori_loop(0, seq // gr, group_body, 0)

    def chunk_body(n, st):
        rows = pl.ds(pl.multiple_of(n * c, c), c)
        inter = lax.dot_general(qd_s[rows, :], st.astype(BF16), (((1,), (1,)), ((), ())),
                                preferred_element_type=F32)
        o = intra_s[rows, :] + inter
        y = o * lax.rsqrt(jnp.mean(o * o, axis=-1, keepdims=True) + EPS) * g_head
        o_ref[rows, :] = (y * _silu(og_ref[rows, :])).astype(o_ref.dtype)
        upd_t = jnp.dot(i_ref[rows, :].T.astype(BF16), ke_s[rows, :], preferred_element_type=F32)
        return dec_s[pl.ds(n, 1), :] * st + upd_t

    lax.fori_loop(0, seq // c, chunk_body, jnp.zeros((HG_DIM, HG_DIM), F32), unroll=8)


def _hgrn(z_hg, lower_bounds, g_head, layer):
    bsz, s, _ = z_hg.shape
    assert s % HG_GROUP_ROWS == 0 and (HG_GROUP_ROWS // HG_CHUNK) % SUBLANES == 0
    depth1 = lower_bounds.shape[0]
    col = lambda sec: (lambda b, h: (b, 0, sec * HG_HEADS + h))
    specs = [pl.BlockSpec((None, s, HG_DIM), col(sec)) for sec in range(4)]
    return pl.pallas_call(
        functools.partial(_hgrn_kernel, layer=layer, seq=s),
        grid=(bsz, HG_HEADS),
        in_specs=[pl.BlockSpec((depth1, HG_HEADS, HG_DIM), lambda b, h: (0, 0, 0)),
                  pl.BlockSpec((HG_HEADS, HG_DIM), lambda b, h: (0, 0))] + specs,
        out_specs=pl.BlockSpec((None, s, HG_DIM), lambda b, h: (b, 0, h)),
        out_shape=jax.ShapeDtypeStruct((bsz, s, HG_HEADS * HG_DIM), BF16),
        scratch_shapes=[pltpu.VMEM((s, HG_DIM), BF16),
                        pltpu.VMEM((s, HG_DIM), BF16),
                        pltpu.VMEM((s, HG_DIM), F32),
                        pltpu.VMEM((s // HG_CHUNK, HG_DIM), F32)],
        compiler_params=_params(("arbitrary", "arbitrary"), 48),
        name="hgrn2",
    )(lower_bounds.reshape(depth1, HG_HEADS, HG_DIM), g_head, z_hg, z_hg, z_hg, z_hg)


def _merge_kernel(ya_ref, yh_ref, ga_ref, gh_ref, x_ref, mod_ref, g2_ref, wa_ref, wh_ref, wo_ref, wr_ref, br_ref,
                  x1_ref, h3_ref, cw3_ref):
    a = jnp.dot(ya_ref[...], wa_ref[...], preferred_element_type=F32)
    hg = jnp.dot(yh_ref[...], wh_ref[...], preferred_element_type=F32)
    merged = _sigmoid(ga_ref[...]) * a + _sigmoid(gh_ref[...]) * hg
    x1 = x_ref[...] + mod_ref[2:3, :] * jnp.dot(merged.astype(BF16), wo_ref[...], preferred_element_type=F32)
    x1_ref[...] = x1
    h2 = _ada_norm(x1, g2_ref[...], mod_ref[3:4, :], mod_ref[4:5, :])
    for t in range(h2.shape[1] // LANES):
        h3_ref[:, t, :] = h2[:, t * LANES:(t + 1) * LANES]

    wr = wr_ref[...]
    h_hi, w_hi = h2.astype(BF16), wr.astype(BF16)
    h_lo, w_lo = (h2 - h_hi.astype(F32)).astype(BF16), (wr - w_hi.astype(F32)).astype(BF16)
    lg = (jnp.dot(h_hi, w_hi, preferred_element_type=F32)
          + (jnp.dot(h_lo, w_hi, preferred_element_type=F32) + jnp.dot(h_hi, w_lo, preferred_element_type=F32))
          + br_ref[...])
    lane = lax.broadcasted_iota(I32, lg.shape, 1)
    neg = -jnp.inf
    is_grp = jnp.logical_and(lane >= N_EXPERTS, lane < N_EXPERTS + N_GROUPS)
    g_lg = jnp.where(is_grp, lg, neg)
    g_max = jnp.max(g_lg, axis=-1, keepdims=True)
    p_grp = 1.0 / jnp.sum(jnp.exp(g_lg - g_max), axis=-1, keepdims=True)
    grp = jnp.min(jnp.where(g_lg == g_max, lane, 2 * LANES), axis=-1, keepdims=True) - N_EXPERTS
    lo = grp * EXPERTS_PER_GROUP
    in_grp = jnp.logical_and(lane >= lo, lane < lo + EXPERTS_PER_GROUP)
    e_lg = jnp.where(in_grp, lg, neg)
    e_max = jnp.max(e_lg, axis=-1, keepdims=True)
    idx1 = jnp.min(jnp.where(e_lg == e_max, lane, 2 * LANES), axis=-1, keepdims=True)
    e_lg2 = jnp.where(lane == idx1, neg, e_lg)
    e_max2 = jnp.max(e_lg2, axis=-1, keepdims=True)
    idx2 = jnp.min(jnp.where(e_lg2 == e_max2, lane, 2 * LANES), axis=-1, keepdims=True)
    e2 = jnp.exp(e_max2 - e_max)
    p1 = p_grp / (1.0 + e2)
    p2 = p_grp * e2 / (1.0 + e2)
    cw = jnp.where(lane == idx1, p1, jnp.where(lane == idx2, p2, jnp.where(lane == grp + N_EXPERTS, 1.0, 0.0)))
    cw3_ref[...] = jnp.zeros_like(cw3_ref)
    cw3_ref[:, 0, :] = cw


def _merge(y_att, y_hg, z_gate, gate_blk, x, mod, g2, wa, wh, wo, wr, br):
    bsz, s, d = x.shape
    tm = min(256, s)
    dh = y_att.shape[-1]
    const = lambda shape: pl.BlockSpec(shape, lambda b, i: (0,) * len(shape))
    row = lambda width, j=0: pl.BlockSpec((None, tm, width), lambda b, i: (b, i, j))
    slab = lambda rows: pl.BlockSpec((tm, rows, LANES), lambda b, i: (b * (s // tm) + i, 0, 0))
    return pl.pallas_call(
        _merge_kernel,
        grid=(bsz, s // tm),
        in_specs=[row(dh), row(dh), row(d, gate_blk), row(d, gate_blk + 1), row(d),
                  pl.BlockSpec((None, 6, d), lambda b, i: (b, 0, 0)), const((1, d)),
                  const((dh, d)), const((dh, d)), const((d, d)), const((d, LANES)), const((1, LANES))],
        out_specs=[row(d), slab(d // LANES), slab(SUBLANES)],
        out_shape=[jax.ShapeDtypeStruct((bsz, s, d), F32),
                   jax.ShapeDtypeStruct((bsz * s, d // LANES, LANES), F32),
                   jax.ShapeDtypeStruct((bsz * s, SUBLANES, LANES), F32)],
        compiler_params=_params(("arbitrary", "arbitrary"), 56),
        name="merge_router",
    )(y_att, y_hg, z_gate, z_gate, x, mod, g2.reshape(1, d), wa, wh, wo, wr, br)


def _moe_kernel(tg_ref, nv_ref, src_ref, h3_hbm, cw3_hbm, wg_ref, wu_ref, wd_ref, y3_hbm,
                hbuf, cbuf, xb, cwv, acc, ybuf, gsem, ssem, *, tm):
    k = pl.program_id(0)
    e = pl.program_id(1)
    nt = pl.num_programs(0)
    ne = pl.num_programs(1)
    slot = k % 2
    nv = nv_ref[k]
    n_slab = xb.shape[1] // LANES

    def gather_copies(tile, slot_, r):
        tok = src_ref[tile * tm + r]
        return (pltpu.make_async_copy(h3_hbm.at[tok], hbuf.at[slot_, r], gsem.at[slot_]),
                pltpu.make_async_copy(cw3_hbm.at[tok], cbuf.at[slot_, r], gsem.at[slot_]))

    def gather(tile, slot_, start):
        def body(r, carry):
            for cp in gather_copies(tile, slot_, r):
                cp.start() if start else cp.wait()
            return carry

        lax.fori_loop(0, tm, body, 0, unroll=8)

    def scatter(tile, start):
        def body(r, carry):
            cp = pltpu.make_async_copy(ybuf.at[r], y3_hbm.at[src_ref[tile * tm + r]], ssem.at[0])
            cp.start() if start else cp.wait()
            return carry

        lax.fori_loop(0, nv_ref[tile], body, 0)

    @pl.when(e == 0)
    def _():
        @pl.when(jnp.logical_and(k == 0, nv > 0))
        def _():
            gather(k, slot, True)

        nxt = jnp.minimum(k + 1, nt - 1)

        @pl.when(jnp.logical_and(k + 1 < nt, nv_ref[nxt] > 0))
        def _():
            gather(nxt, 1 - slot, True)

        @pl.when(nv > 0)
        def _():
            gather(k, slot, False)
            for t in range(n_slab):
                xb[:, t * LANES:(t + 1) * LANES] = hbuf[slot, :, t, :].astype(BF16)
            cwv[...] = cbuf[slot, :, 0, :]
            acc[...] = jnp.zeros_like(acc)

    @pl.when(nv > 0)
    def _():
        h = xb[...]
        a = jnp.dot(h, wg_ref[...].astype(BF16), preferred_element_type=F32)
        u = jnp.dot(h, wu_ref[...].astype(BF16), preferred_element_type=F32)
        cw = cwv[...]
        lane = lax.broadcasted_iota(I32, cw.shape, 1)
        cwe = jnp.sum(jnp.where(lane == tg_ref[k] * ne + e, cw, 0.0), axis=-1, keepdims=True)
        hm = (_silu(a) * u * cwe).astype(BF16)
        acc[...] += jnp.dot(hm, wd_ref[...].astype(BF16), preferred_element_type=F32)

    @pl.when(e == ne - 1)
    def _():
        @pl.when(k > 0)
        def _():
            scatter(jnp.maximum(k - 1, 0), False)

        @pl.when(nv > 0)
        def _():
            for t in range(n_slab):
                ybuf[:, t, :] = acc[:, t * LANES:(t + 1) * LANES]
            scatter(k, True)

        @pl.when(k == nt - 1)
        def _():
            scatter(k, False)


def _moe_tiles(grp, tm):
    t = grp.shape[0]
    nt = t // tm + N_GROUPS
    order = jnp.argsort(grp, stable=True).astype(I32)
    counts = jnp.sum((grp[:, None] == jnp.arange(N_GROUPS, dtype=I32)[None, :]).astype(I32), axis=0)
    tiles_g = (counts + tm - 1) // tm
    tile_end = jnp.cumsum(tiles_g)
    tile0 = tile_end - tiles_g
    gstart = jnp.cumsum(counts) - counts
    k = jnp.arange(nt, dtype=I32)
    used = k < tile_end[-1]
    tg = jnp.sum((k[:, None] >= tile_end[None, :]).astype(I32), axis=1)
    last_g = jnp.sum((tile_end[-1] - 1 >= tile_end).astype(I32))
    tg = jnp.where(used, tg, last_g)
    q0 = (k - tile0[tg]) * tm
    nvalid = jnp.where(used, jnp.clip(counts[tg] - q0, 0, tm), 0).astype(I32)
    r = jnp.arange(tm, dtype=I32)
    idx = jnp.clip(gstart[tg][:, None] + q0[:, None] + r[None, :], 0, t - 1)
    src = jnp.where(r[None, :] < nvalid[:, None], order[idx], 0).astype(I32)
    return tg.astype(I32), nvalid, src.reshape(nt * tm)


def _moe(h3, cw3, wg, wu, wd):
    t, n_slab, _ = h3.shape
    d = n_slab * LANES
    ne_all, _, ff = wg.shape
    tm = min(512, t)
    grp = jnp.argmax(cw3[:, 0, N_EXPERTS:N_EXPERTS + N_GROUPS], axis=-1).astype(I32)
    tg, nvalid, src = _moe_tiles(grp, tm)
    nt = tg.shape[0]
    w_idx = lambda k, e, tg_r, nv_r, src_r: (tg_r[k] * EXPERTS_PER_GROUP + jnp.where(nv_r[k] > 0, e, EXPERTS_PER_GROUP - 1),
                                           0, 0)
    grid_spec = pltpu.PrefetchScalarGridSpec(
        num_scalar_prefetch=3,
        grid=(nt, EXPERTS_PER_GROUP),
        in_specs=[pl.BlockSpec(memory_space=pl.ANY), pl.BlockSpec(memory_space=pl.ANY),
                  pl.BlockSpec((None, d, ff), w_idx), pl.BlockSpec((None, d, ff), w_idx),
                  pl.BlockSpec((None, ff, d), w_idx)],
        out_specs=pl.BlockSpec(memory_space=pl.ANY),
        scratch_shapes=[pltpu.VMEM((2, tm, n_slab, LANES), F32),
                        pltpu.VMEM((2, tm, SUBLANES, LANES), F32),
                        pltpu.VMEM((tm, d), BF16),
                        pltpu.VMEM((tm, LANES), F32),
                        pltpu.VMEM((tm, d), F32),
                        pltpu.VMEM((tm, n_slab, LANES), F32),
                        pltpu.SemaphoreType.DMA((2,)),
                        pltpu.SemaphoreType.DMA((1,))])
    return pl.pallas_call(
        functools.partial(_moe_kernel, tm=tm),
        grid_spec=grid_spec,
        out_shape=jax.ShapeDtypeStruct((t, n_slab, LANES), F32),
        compiler_params=_params(("arbitrary", "arbitrary"), 56),
        name="moe_experts",
    )(tg, nvalid, src, h3, cw3, wg, wu, wd)


def _final_kernel(x1_ref, y3_ref, mod_ref, g_ref, o_ref, *, gate_row):
    y = jnp.concatenate([y3_ref[:, t, :] for t in range(y3_ref.shape[1])], axis=1)
    x2 = x1_ref[...] + mod_ref[gate_row:gate_row + 1, :] * y
    ms = jnp.mean(x2 * x2, axis=-1, keepdims=True)
    o_ref[...] = x2 * lax.rsqrt(ms + EPS) * g_ref[...]


def _final(x1, y3, mod, g):
    bsz, s, d = x1.shape
    tm = min(512, s)
    return pl.pallas_call(
        functools.partial(_final_kernel, gate_row=5),
        grid=(bsz, s // tm),
        in_specs=[pl.BlockSpec((None, tm, d), lambda b, i: (b, i, 0)),
                  pl.BlockSpec((tm, d // LANES, LANES), lambda b, i: (b * (s // tm) + i, 0, 0)),
                  pl.BlockSpec((None, 6, d), lambda b, i: (b, 0, 0)),
                  pl.BlockSpec((1, d), lambda b, i: (0, 0))],
        out_specs=pl.BlockSpec((None, tm, d), lambda b, i: (b, i, 0)),
        out_shape=jax.ShapeDtypeStruct((bsz, s, d), F32),
        compiler_params=_params(("arbitrary", "arbitrary"), 40),
        name="residual_final_norm",
    )(x1, y3, mod, g.reshape(1, d))


def kernel(x, c, positions, w_ada, b_ada, g_norm1, w_in, g_head, hg_lower_bounds, w_attn_up, w_hgrn_up, w_out,
           g_norm2, w_router_group, b_router_group, w_router_expert, b_router_expert, w_exp_gate, w_exp_up,
           w_exp_down, g_final):
    depth = w_ada.shape[0]
    assert depth == 1
    hg_w = HG_HEADS * HG_DIM
    for l in range(depth):
        mod = _modulation(c, w_ada[l], b_ada[l])
        w_att, w_rest = _prep_w_in(w_in, l)
        z_att = _norm_matmul(x, mod, g_norm1[l], w_att, ATT_WIDTH)
        z_rest = _norm_matmul(x, mod, g_norm1[l], w_rest, 1024)
        y_att = _attention(z_att, positions)
        y_hg = _hgrn(z_rest, hg_lower_bounds, g_head[l], l)
        w_r = jnp.pad(jnp.concatenate([w_router_expert[l], w_router_group[l]], axis=1),
                      ((0, 0), (0, LANES - N_EXPERTS - N_GROUPS)))
        b_r = jnp.pad(jnp.concatenate([b_router_expert[l], b_router_group[l]]),
                      (0, LANES - N_EXPERTS - N_GROUPS)).reshape(1, LANES)
        x1, h3, cw3 = _merge(y_att, y_hg, z_rest, 4 * hg_w // x.shape[-1], x, mod, g_norm2[l],
                             w_attn_up[l].astype(BF16), w_hgrn_up[l].astype(BF16), w_out[l].astype(BF16), w_r, b_r)
        y3 = _moe(h3, cw3, w_exp_gate[l], w_exp_up[l], w_exp_down[l])
        x = _final(x1, y3, mod, g_final)
    return x
```
